```python
import math
import jax
import jax.numpy as jnp
from jax import lax
import numpy as np

D_MODEL = 2048
BATCH = 2
SEQ = 4096
DEPTH = 2
DEC_BATCH = 128
DEC_SEQ = 4
PAST_LEN = 2048
PAGE_SIZE = 128

A_HEADS = 8
A_QK = 64
A_V = 2 * A_QK
A_WIDTH = A_HEADS * A_V
ATT_SCALE = A_QK ** -0.5
Q_BLOCK = 128
M_HEADS = 4
M_WIDTH = D_MODEL // 2
M_DH = M_WIDTH // M_HEADS
CONV_W = 4
CHUNK = 64
EPS = 1e-6
P_IN = 4 * A_WIDTH + 5 * M_WIDTH + 2 * M_HEADS + 2 * D_MODEL

kernel_name = 'hybrid_diffattn_mlstm_gated_decoder_step'


def _split_sizes():
    return (A_WIDTH, A_WIDTH, A_WIDTH, A_WIDTH,
            M_WIDTH, M_WIDTH, M_WIDTH, M_WIDTH,
            M_HEADS, M_HEADS, M_WIDTH,
            D_MODEL, D_MODEL)


def rms_norm(x, w):
    xf = x.astype(jnp.float32)
    y = xf * lax.rsqrt(jnp.mean(xf * xf, axis=-1, keepdims=True) + EPS)
    return (y * w.astype(jnp.float32)).astype(x.dtype)


def head_norm(x, w, center):
    xf = x.astype(jnp.float32)
    if center:
        xf = xf - jnp.mean(xf, axis=-1, keepdims=True)
    y = xf * lax.rsqrt(jnp.mean(xf * xf, axis=-1, keepdims=True) + EPS)
    return y.reshape(*x.shape[:-2], -1) * w.astype(jnp.float32)


def diff_weights(s, lam):
    p = jax.nn.softmax(s, axis=-1)
    return p[:, :, 0] - lam * p[:, :, 1]


def prompt_attend(q, k, v, lam):
    b, t = q.shape[:2]
    nb = t // Q_BLOCK
    qb = jnp.moveaxis(q.reshape(b, nb, Q_BLOCK, *q.shape[2:]), 1, 0)
    kpos = jnp.arange(t)

    def block(args):
        qi, i = args
        qpos = i * Q_BLOCK + jnp.arange(Q_BLOCK)
        s = jnp.einsum('bqhmd,bkhmd->bhmqk', qi, k).astype(jnp.float32) * ATT_SCALE
        s = jnp.where(qpos[:, None] >= kpos[None, :], s, -jnp.inf)
        w = diff_weights(s, lam).astype(v.dtype)
        return jnp.einsum('bhqk,bkhd->bqhd', w, v)

    out = lax.map(block, (qb, jnp.arange(nb)))
    return jnp.moveaxis(out, 0, 1).reshape(b, t, A_HEADS, A_V)


def make_sample_attend(past_k, past_v):
    def attend(q, k, v, lam):
        t = q.shape[1]
        s_past = jnp.einsum('bqhmd,bkhmd->bhmqk', q, past_k).astype(jnp.float32)
        s_new = jnp.einsum('bqhmd,bkhmd->bhmqk', q, k).astype(jnp.float32)
        causal = jnp.tril(jnp.ones((t, t), dtype=bool))
        s = jnp.concatenate([s_past, jnp.where(causal, s_new, -jnp.inf)], axis=-1) * ATT_SCALE
        w = diff_weights(s, lam).astype(v.dtype)
        n_past = past_k.shape[1]
        return (jnp.einsum('bhqk,bkhd->bqhd', w[..., :n_past], past_v)
                + jnp.einsum('bhqk,bkhd->bqhd', w[..., n_past:], v))
    return attend


def causal_conv(u, buf, w, bias):
    t = u.shape[1]
    full = jnp.concatenate([buf.astype(u.dtype), u], axis=1)
    out = bias + sum(full[:, j:j + t] * w[j] for j in range(CONV_W))
    return out, full[:, t:]


def mlstm_chunkwise(q, k, v, log_i, log_f, C0, n0, m0):
    b, t, h, dh = q.shape
    L = CHUNK if t % CHUNK == 0 else t
    nc = t // L

    def to_chunks(a):
        return jnp.moveaxis(a.reshape(b, nc, L, *a.shape[2:]), 1, 0)

    causal = jnp.tril(jnp.ones((L, L), dtype=bool))

    def step(carry, xs):
        C, n, m = carry
        qc, kc, vc, ic, fc = xs
        bt = jnp.cumsum(fc, axis=1).transpose(0, 2, 1)
        it = ic.transpose(0, 2, 1)
        logD = bt[:, :, :, None] - bt[:, :, None, :] + it[:, :, None, :]
        logD = jnp.where(causal, logD, -jnp.inf)
        log_inter = bt + m[:, :, None]
        m_t = jnp.maximum(log_inter, jnp.max(logD, axis=-1))
        D = jnp.exp(logD - m_t[..., None])
        inter = jnp.exp(log_inter - m_t)
        s = jnp.einsum('blhd,bshd->bhls', qc, kc) * D
        num = (jnp.einsum('bhls,bshe->blhe', s, vc)
               + jnp.einsum('blhd,bhde->blhe', qc, C) * inter.transpose(0, 2, 1)[..., None])
        den = jnp.sum(s, axis=-1) + inter * jnp.einsum('blhd,bhd->bhl', qc, n)
        den = jnp.maximum(jnp.abs(den), jnp.exp(-m_t))
        hc = num / den.transpose(0, 2, 1)[..., None]
        b_last = bt[:, :, -1]
        log_w = b_last[:, :, None] - bt + it
        m_new = jnp.maximum(b_last + m, jnp.max(log_w, axis=-1))
        wk = jnp.exp(log_w - m_new[..., None])
        decay = jnp.exp(b_last + m - m_new)
        C_new = decay[..., None, None] * C + jnp.einsum('bhl,blhd,blhe->bhde', wk, kc, vc)
        n_new = decay[..., None] * n + jnp.einsum('bhl,blhd->bhd', wk, kc)
        return (C_new, n_new, m_new), hc

    (C, n, m), hs = lax.scan(step, (C0, n0, m0),
                             (to_chunks(q), to_chunks(k), to_chunks(v),
                              to_chunks(log_i), to_chunks(log_f)))
    return jnp.moveaxis(hs, 0, 1).reshape(b, t, h, dh), C, n, m


def mixer(h, attend, C0, n0, m0, conv_buf, lam, lam_init,
          w_in, b_if, conv_w, conv_b, attn_norm_w, mlstm_norm_w, w_pa, w_pm, w_out):
    f32 = jnp.float32
    b, t, _ = h.shape
    u = h @ w_in
    idx = np.cumsum(_split_sizes())[:-1].tolist()
    aq, ak, av, az, mq, mk, mv, mo, mi, mf, mz, ga, gm = jnp.split(u, idx, axis=-1)
    att = attend(aq.reshape(b, t, A_HEADS, 2, A_QK), ak.reshape(b, t, A_HEADS, 2, A_QK),
                 av.reshape(b, t, A_HEADS, A_V), lam)
    att = head_norm(att, attn_norm_w, False) * (1.0 - lam_init)
    o_a = att.astype(h.dtype) * jax.nn.silu(az)
    qk, new_buf = causal_conv(jnp.concatenate([mq, mk], axis=-1), conv_buf, conv_w, conv_b)
    qk = jax.nn.silu(qk).astype(f32)
    q = qk[..., :M_WIDTH].reshape(b, t, M_HEADS, M_DH)
    k = qk[..., M_WIDTH:].reshape(b, t, M_HEADS, M_DH) * (M_DH ** -0.5)
    v = mv.astype(f32).reshape(b, t, M_HEADS, M_DH)
    log_i = mi.astype(f32) + b_if[:M_HEADS].astype(f32)
    log_f = jax.nn.log_sigmoid(mf.astype(f32) + b_if[M_HEADS:].astype(f32))
    hm, C, n, m = mlstm_chunkwise(q, k, v, log_i, log_f,
                                  C0.astype(f32), n0.astype(f32), m0.astype(f32))
    o_m = (head_norm(hm, mlstm_norm_w, True) * jax.nn.sigmoid(mo.astype(f32))).astype(h.dtype)
    o_m = o_m * jax.nn.silu(mz)
    y = jax.nn.sigmoid(ga) * (o_a @ w_pa) + jax.nn.sigmoid(gm) * (o_m @ w_pm)
    k_rows = ak.reshape(b, t, A_HEADS, 2 * A_QK)
    v_rows = av.reshape(b, t, A_HEADS, A_V)
    return y @ w_out, k_rows, v_rows, C, n, m, new_buf


def setup_inputs(seed: int = 0) -> dict:
    key = jax.random.key(seed)
    ks = jax.random.split(key, 24)
    f32 = jnp.float32

    def nrm(k, shape, s):
        return jax.random.normal(k, shape, f32) * s

    n_pages = PAST_LEN // PAGE_SIZE
    n_used = DEC_BATCH * n_pages
    n_pool = n_used + max(1, n_used // 4)
    page_table = jax.random.permutation(ks[8], n_pool)[:n_used].reshape(DEC_BATCH, n_pages).astype(jnp.int32)
    b_if = jnp.concatenate([nrm(ks[13], (DEPTH, M_HEADS), 0.1),
                            3.0 + nrm(ks[14], (DEPTH, M_HEADS), 0.5)], axis=-1)
    return {
        'x_prompt': nrm(ks[0], (BATCH, SEQ, D_MODEL), 1.0),
        'x_sample': nrm(ks[1], (DEC_BATCH, DEC_SEQ, D_MODEL), 1.0),
        'cache_k': nrm(ks[2], (DEPTH, n_pool, PAGE_SIZE, A_HEADS, 2 * A_QK), 1.0),
        'cache_v': nrm(ks[3], (DEPTH, n_pool, PAGE_SIZE, A_HEADS, A_V), 1.0),
        'state_C': nrm(ks[4], (DEPTH, DEC_BATCH, M_HEADS, M_DH, M_DH), 0.05),
        'state_n': nrm(ks[5], (DEPTH, DEC_BATCH, M_HEADS, M_DH), 0.5),
        'state_m': nrm(ks[6], (DEPTH, DEC_BATCH, M_HEADS), 1.0),
        'state_conv': nrm(ks[7], (DEPTH, DEC_BATCH, CONV_W - 1, 2 * M_WIDTH), 1.0),
        'page_table': page_table,
        'norm_pre': 1.0 + nrm(ks[9], (DEPTH, D_MODEL), 0.02),
        'norm_post': 1.0 + nrm(ks[10], (DEPTH, D_MODEL), 0.02),
        'w_in': nrm(ks[11], (DEPTH, D_MODEL, P_IN), D_MODEL ** -0.5),
        'b_if': b_if,
        'conv_w': nrm(ks[12], (DEPTH, CONV_W, 2 * M_WIDTH), CONV_W ** -0.5),
        'conv_b': nrm(ks[15], (DEPTH, 2 * M_WIDTH), 0.01),
        'lambda_qk': nrm(ks[16], (DEPTH, 4, A_QK), 0.1),
        'attn_norm_w': 1.0 + nrm(ks[17], (DEPTH, A_WIDTH), 0.02),
        'mlstm_norm_w': 1.0 + nrm(ks[18], (DEPTH, M_WIDTH), 0.02),
        'w_pa': nrm(ks[19], (DEPTH, A_WIDTH, D_MODEL), A_WIDTH ** -0.5),
        'w_pm': nrm(ks[20], (DEPTH, M_WIDTH, D_MODEL), M_WIDTH ** -0.5),
        'w_out': nrm(ks[21], (DEPTH, D_MODEL, D_MODEL), D_MODEL ** -0.5),
    }


def reference(x_prompt, x_sample, cache_k, cache_v, state_C, state_n, state_m, state_conv,
              page_table, norm_pre, norm_post, w_in, b_if, conv_w, conv_b, lambda_qk,
              attn_norm_w, mlstm_norm_w, w_pa, w_pm, w_out):
    f32 = jnp.float32
    bp = x_prompt.shape[0]
    bd = x_sample.shape[0]
    n_past = page_table.shape[1] * PAGE_SIZE
    xp, xs = x_prompt, x_sample
    kp_l, vp_l, Cp_l, np_l, mp_l, cp_l = [], [], [], [], [], []
    ks_l, vs_l, Cs_l, ns_l, ms_l, cs_l = [], [], [], [], [], []
    for l in range(DEPTH):
        lam_init = 0.8 - 0.6 * math.exp(-0.3 * l)
        lq = lambda_qk[l].astype(f32)
        lam = jnp.exp(jnp.dot(lq[0], lq[1])) - jnp.exp(jnp.dot(lq[2], lq[3])) + lam_init
        params = (w_in[l], b_if[l], conv_w[l], conv_b[l], attn_norm_w[l], mlstm_norm_w[l],
                  w_pa[l], w_pm[l], w_out[l])
        out, k_r, v_r, C, n, m, buf = mixer(
            rms_norm(xp, norm_pre[l]), prompt_attend,
            jnp.zeros((bp, M_HEADS, M_DH, M_DH), f32), jnp.zeros((bp, M_HEADS, M_DH), f32),
            jnp.zeros((bp, M_HEADS), f32), jnp.zeros((bp, CONV_W - 1, 2 * M_WIDTH), xp.dtype),
            lam, lam_init, *params)
        xp = xp + rms_norm(out, norm_post[l])
        kp_l.append(k_r); vp_l.append(v_r); Cp_l.append(C); np_l.append(n); mp_l.append(m); cp_l.append(buf)
        past_k = cache_k[l][page_table].reshape(bd, n_past, A_HEADS, 2, A_QK)
        past_v = cache_v[l][page_table].reshape(bd, n_past, A_HEADS, A_V)
        out, k_r, v_r, C, n, m, buf = mixer(
            rms_norm(xs, norm_pre[l]), make_sample_attend(past_k, past_v),
            state_C[l], state_n[l], state_m[l], state_conv[l], lam, lam_init, *params)
        xs = xs + rms_norm(out, norm_post[l])
        ks_l.append(k_r); vs_l.append(v_r); Cs_l.append(C); ns_l.append(n); ms_l.append(m); cs_l.append(buf)
    k_prompt = jnp.stack(kp_l)
    v_prompt = jnp.stack(vp_l)
    C_prompt = jnp.stack(Cp_l)
    n_prompt = jnp.stack(np_l)
    m_prompt = jnp.stack(mp_l)
    conv_prompt = jnp.stack(cp_l)
    k_sample = jnp.stack(ks_l)
    v_sample = jnp.stack(vs_l)
    C_sample = jnp.stack(Cs_l)
    n_sample = jnp.stack(ns_l)
    m_sample = jnp.stack(ms_l)
    conv_sample = jnp.stack(cs_l)
    return (xp, xs, k_prompt, v_prompt, C_prompt, n_prompt, m_prompt, conv_prompt,
            k_sample, v_sample, C_sample, n_sample, m_sample, conv_sample)
```

```python
import functools
import math

import jax
import jax.numpy as jnp
import numpy as np
from jax import lax
from jax.experimental import pallas as pl
from jax.experimental.pallas import tpu as pltpu

F32 = jnp.float32
BF16 = jnp.bfloat16

EPS = 1e-6
CHUNK = 64
LANES = 128
NEG_BIG = -1e30
VMEM_LIMIT = 56 * 1024 * 1024


def _sigmoid(x):
    return 1.0 / (1.0 + jnp.exp(-x))


def _silu(x):
    return x * _sigmoid(x)


def _params(sem):
    return pltpu.CompilerParams(dimension_semantics=sem, vmem_limit_bytes=VMEM_LIMIT)


def _inproj_kernel(x_ref, nw_ref, w_ref, wif_ref, u_ref, uif_ref, h_scr):
    @pl.when(pl.program_id(1) == 0)
    def _():
        x = x_ref[...]
        ms = jnp.mean(x * x, axis=-1, keepdims=True)
        h = (x * lax.rsqrt(ms + EPS) * nw_ref[...]).astype(BF16)
        h_scr[...] = h
        uif_ref[...] = jnp.dot(h, wif_ref[...], preferred_element_type=F32)

    u_ref[...] = jnp.dot(h_scr[...], w_ref[...], preferred_element_type=F32)


def in_projection(x, norm_w, w_main, w_if, *, tm, tn):
    t, d = x.shape
    n = w_main.shape[1]
    return pl.pallas_call(
        _inproj_kernel,
        grid=(t // tm, n // tn),
        in_specs=[
            pl.BlockSpec((tm, d), lambda i, j: (i, 0)),
            pl.BlockSpec((1, d), lambda i, j: (0, 0)),
            pl.BlockSpec((d, tn), lambda i, j: (0, j)),
            pl.BlockSpec((d, LANES), lambda i, j: (0, 0)),
        ],
        out_specs=[
            pl.BlockSpec((tm, tn), lambda i, j: (i, j)),
            pl.BlockSpec((tm, LANES), lambda i, j: (i, 0)),
        ],
        out_shape=[jax.ShapeDtypeStruct((t, n), F32), jax.ShapeDtypeStruct((t, LANES), F32)],
        scratch_shapes=[pltpu.VMEM((tm, d), BF16)],
        compiler_params=_params(("parallel", "arbitrary")),
        name="in_projection",
    )(x, norm_w.reshape(1, d), w_main, w_if)


def _merge_kernel(oa_ref, om_ref, wpa_ref, wpm_ref, ga_ref, gm_ref, y_ref):
    pa = jnp.dot(oa_ref[...], wpa_ref[...], preferred_element_type=F32)
    pm = jnp.dot(om_ref[...], wpm_ref[...], preferred_element_type=F32)
    y = _sigmoid(ga_ref[...]) * pa + _sigmoid(gm_ref[...]) * pm
    y_ref[...] = y.astype(y_ref.dtype)


def gated_merge(o_a, o_m, w_pa, w_pm, u, ga_col, gm_col, *, tm, tn):
    t, wa = o_a.shape
    wm = o_m.shape[1]
    d = w_pa.shape[1]
    ga_blk, gm_blk = ga_col // tn, gm_col // tn
    return pl.pallas_call(
        _merge_kernel,
        grid=(t // tm, d // tn),
        in_specs=[
            pl.BlockSpec((tm, wa), lambda i, j: (i, 0)),
            pl.BlockSpec((tm, wm), lambda i, j: (i, 0)),
            pl.BlockSpec((wa, tn), lambda i, j: (0, j)),
            pl.BlockSpec((wm, tn), lambda i, j: (0, j)),
            pl.BlockSpec((tm, tn), lambda i, j: (i, ga_blk + j)),
            pl.BlockSpec((tm, tn), lambda i, j: (i, gm_blk + j)),
        ],
        out_specs=pl.BlockSpec((tm, tn), lambda i, j: (i, j)),
        out_shape=jax.ShapeDtypeStruct((t, d), BF16),
        compiler_params=_params(("parallel", "arbitrary")),
        name="gated_merge",
    )(o_a, o_m, w_pa, w_pm, u, u)


def _outproj_kernel(y_ref, w_ref, nw_ref, x_ref, o_ref):
    out = jnp.dot(y_ref[...], w_ref[...], preferred_element_type=F32)
    ms = jnp.mean(out * out, axis=-1, keepdims=True)
    o_ref[...] = x_ref[...] + out * lax.rsqrt(ms + EPS) * nw_ref[...]


def out_projection(y, w_out, norm_w, x, *, tm):
    t, d = x.shape
    return pl.pallas_call(
        _outproj_kernel,
        grid=(t // tm,),
        in_specs=[
            pl.BlockSpec((tm, d), lambda i: (i, 0)),
            pl.BlockSpec((d, d), lambda i: (0, 0)),
            pl.BlockSpec((1, d), lambda i: (0, 0)),
            pl.BlockSpec((tm, d), lambda i: (i, 0)),
        ],
        out_specs=pl.BlockSpec((tm, d), lambda i: (i, 0)),
        out_shape=jax.ShapeDtypeStruct((t, d), F32),
        compiler_params=_params(("parallel",)),
        name="out_projection",
    )(y, w_out, norm_w.reshape(1, d), x)


def _prompt_attn_kernel(scal_ref, q_ref, k_ref, v_ref, z_ref, nw_ref, o_ref, kb_scr, vb_scr,
                        *, tq, dqk, scale, post_scale):
    qi = pl.program_id(2)
    lam = scal_ref[0]

    @pl.when(qi == 0)
    def _():
        kb_scr[...] = k_ref[0].astype(BF16)
        vb_scr[...] = v_ref[0].astype(BF16)

    q = q_ref[0] * scale
    lane = lax.broadcasted_iota(jnp.int32, q.shape, 1)
    q1 = jnp.where(lane < dqk, q, 0.0).astype(BF16)
    q2 = jnp.where(lane >= dqk, q, 0.0).astype(BF16)
    dv = v_ref.shape[-1]

    def scores(qm, k):
        return lax.dot_general(qm, k, (((1,), (1,)), ((), ())), preferred_element_type=F32)

    def online(state, s, v):
        m, l, acc = state
        m_new = jnp.maximum(m, jnp.max(s, axis=-1, keepdims=True))
        alpha = jnp.exp(m - m_new)
        p = jnp.exp(s - m_new)
        l = alpha * l + jnp.sum(p, axis=-1, keepdims=True)
        acc = alpha * acc + jnp.dot(p.astype(BF16), v, preferred_element_type=F32)
        return m_new, l, acc

    def block(j, carry, masked):
        st1, st2 = carry
        off = pl.multiple_of(j * tq, tq)
        k = kb_scr[pl.ds(off, tq), :]
        v = vb_scr[pl.ds(off, tq), :]
        s1, s2 = scores(q1, k), scores(q2, k)
        if masked:
            row = lax.broadcasted_iota(jnp.int32, s1.shape, 0)
            col = lax.broadcasted_iota(jnp.int32, s1.shape, 1)
            keep = row >= col
            s1 = jnp.where(keep, s1, -jnp.inf)
            s2 = jnp.where(keep, s2, -jnp.inf)
        return online(st1, s1, v), online(st2, s2, v)

    def init():
        return (jnp.full((tq, 1), NEG_BIG, F32), jnp.zeros((tq, 1), F32), jnp.zeros((tq, dv), F32))

    carry = lax.fori_loop(0, qi, lambda j, c: block(j, c, False), (init(), init()))
    (_, l1, a1), (_, l2, a2) = block(qi, carry, True)

    out = a1 / l1 - lam * (a2 / l2)
    ms = jnp.mean(out * out, axis=-1, keepdims=True)
    y = out * lax.rsqrt(ms + EPS) * nw_ref[...] * post_scale
    o_ref[0] = (y * _silu(z_ref[0])).astype(o_ref.dtype)


def prompt_attention(u, scal, norm_w, *, heads, dqk, q_col, k_col, v_col, z_col, post_scale, tq):
    b, s, _ = u.shape
    dh = 2 * dqk
    qb, kb, vb, zb = q_col // dh, k_col // dh, v_col // dh, z_col // dh
    kern = functools.partial(_prompt_attn_kernel, tq=tq, dqk=dqk, scale=dqk ** -0.5, post_scale=post_scale)
    return pl.pallas_call(
        kern,
        grid=(b, heads, s // tq),
        in_specs=[
            pl.BlockSpec(memory_space=pltpu.SMEM),
            pl.BlockSpec((1, tq, dh), lambda bi, h, i: (bi, i, qb + h)),
            pl.BlockSpec((1, s, dh), lambda bi, h, i: (bi, 0, kb + h)),
            pl.BlockSpec((1, s, dh), lambda bi, h, i: (bi, 0, vb + h)),
            pl.BlockSpec((1, tq, dh), lambda bi, h, i: (bi, i, zb + h)),
            pl.BlockSpec((1, dh), lambda bi, h, i: (0, h)),
        ],
        out_specs=pl.BlockSpec((1, tq, dh), lambda bi, h, i: (bi, i, h)),
        out_shape=jax.ShapeDtypeStruct((b, s, heads * dh), BF16),
        scratch_shapes=[pltpu.VMEM((s, dh), BF16), pltpu.VMEM((s, dh), BF16)],
        compiler_params=_params(("parallel", "parallel", "arbitrary")),
        name="prompt_attention",
    )(scal, u, u, u, u, norm_w.reshape(1, heads * dh))


def _sample_attn_kernel(pt_ref, scal_ref, q_ref, kn_ref, vn_ref, z_ref, nw_ref, bias_ref, biasn_ref, *rest,
                        n_pages, heads, dqk, nq, scale, post_scale):
    del pt_ref
    k_refs = rest[:n_pages]
    v_refs = rest[n_pages:2 * n_pages]
    o_ref = rest[2 * n_pages]
    s_scr = rest[2 * n_pages + 1]
    lam = scal_ref[0]
    dh = 2 * dqk
    rows_pp = k_refs[0].shape[0] * k_refs[0].shape[1]

    q = q_ref[0] * scale
    lane = lax.broadcasted_iota(jnp.int32, (nq, dh), 1)
    pieces = []
    for h in range(heads):
        qh = q[:, h * dh:(h + 1) * dh]
        pieces.append(jnp.where(lane < dqk, qh, 0.0))
        pieces.append(jnp.where(lane >= dqk, qh, 0.0))
    qall = jnp.concatenate(pieces, axis=0).astype(BF16)
    nrow = qall.shape[0]

    def scores(k2d):
        return lax.dot_general(qall, k2d, (((1,), (1,)), ((), ())), preferred_element_type=F32)

    bias = bias_ref[...]
    m = jnp.full((nrow, 1), NEG_BIG, F32)
    for j in range(n_pages):
        kj = k_refs[j][...].reshape(rows_pp, dh).astype(BF16)
        s = scores(kj) + bias
        s_scr[:, j * rows_pp:(j + 1) * rows_pp] = s
        m = jnp.maximum(m, jnp.max(s, axis=-1, keepdims=True))
    kn = kn_ref[0].reshape(nq * heads, dh).astype(BF16)
    sn = scores(kn) + biasn_ref[...]
    m = jnp.maximum(m, jnp.max(sn, axis=-1, keepdims=True))

    l = jnp.zeros((nrow, 1), F32)
    acc = jnp.zeros((nrow, dh), F32)
    for j in range(n_pages):
        p = jnp.exp(s_scr[:, j * rows_pp:(j + 1) * rows_pp] - m)
        l = l + jnp.sum(p, axis=-1, keepdims=True)
        vj = v_refs[j][...].reshape(rows_pp, dh).astype(BF16)
        acc = acc + jnp.dot(p.astype(BF16), vj, preferred_element_type=F32)
    pn = jnp.exp(sn - m)
    l = l + jnp.sum(pn, axis=-1, keepdims=True)
    vn = vn_ref[0].reshape(nq * heads, dh).astype(BF16)
    acc = acc + jnp.dot(pn.astype(BF16), vn, preferred_element_type=F32)

    o = acc / l
    z = z_ref[0]
    nw = nw_ref[...]
    for h in range(heads):
        r0 = h * 2 * nq
        oh = o[r0:r0 + nq] - lam * o[r0 + nq:r0 + 2 * nq]
        ms = jnp.mean(oh * oh, axis=-1, keepdims=True)
        y = oh * lax.rsqrt(ms + EPS) * nw[:, h * dh:(h + 1) * dh] * post_scale
        o_ref[0, :, h * dh:(h + 1) * dh] = (y * _silu(z[:, h * dh:(h + 1) * dh])).astype(o_ref.dtype)


def _sample_masks(heads, nq, page_rows):
    r = np.arange(heads * 2 * nq)
    rh, rt = r // (2 * nq), r % nq
    c = np.arange(page_rows)
    past = np.where((c[None, :] % heads) == rh[:, None], 0.0, -np.inf).astype(np.float32)
    cn = np.arange(nq * heads)
    ok = ((cn[None, :] % heads) == rh[:, None]) & ((cn[None, :] // heads) <= rt[:, None])
    new = np.where(ok, 0.0, -np.inf).astype(np.float32)
    return jnp.asarray(past), jnp.asarray(new)


def sample_attention(u, k_new, v_new, cache_k, cache_v, layer, page_table, scal, norm_w,
                     *, dqk, q_col, z_col, post_scale):
    b, nq, _ = u.shape
    _, _, page, heads, dh = cache_k.shape
    n_pages = page_table.shape[1]
    width = heads * dh
    qb, zb = q_col // width, z_col // width
    bias, bias_new = _sample_masks(heads, nq, page * heads)
    nrow = heads * 2 * nq

    def page_spec(j):
        return pl.BlockSpec((None, None, page, heads, dh), lambda i, pt: (layer, pt[i, j], 0, 0, 0))

    kern = functools.partial(_sample_attn_kernel, n_pages=n_pages, heads=heads, dqk=dqk, nq=nq,
                             scale=dqk ** -0.5, post_scale=post_scale)
    grid_spec = pltpu.PrefetchScalarGridSpec(
        num_scalar_prefetch=1,
        grid=(b,),
        in_specs=[
            pl.BlockSpec(memory_space=pltpu.SMEM),
            pl.BlockSpec((1, nq, width), lambda i, pt: (i, 0, qb)),
            pl.BlockSpec((1, nq, heads, dh), lambda i, pt: (i, 0, 0, 0)),
            pl.BlockSpec((1, nq, heads, dh), lambda i, pt: (i, 0, 0, 0)),
            pl.BlockSpec((1, nq, width), lambda i, pt: (i, 0, zb)),
            pl.BlockSpec((1, width), lambda i, pt: (0, 0)),
            pl.BlockSpec(bias.shape, lambda i, pt: (0, 0)),
            pl.BlockSpec(bias_new.shape, lambda i, pt: (0, 0)),
        ] + [page_spec(j) for j in range(n_pages)] * 2,
        out_specs=pl.BlockSpec((1, nq, width), lambda i, pt: (i, 0, 0)),
        scratch_shapes=[pltpu.VMEM((nrow, n_pages * page * heads), F32)],
    )
    return pl.pallas_call(
        kern,
        grid_spec=grid_spec,
        out_shape=jax.ShapeDtypeStruct((b, nq, width), BF16),
        compiler_params=_params(("arbitrary",)),
        name="sample_attention",
    )(page_table, scal, u, k_new, v_new, u, norm_w.reshape(1, width), bias, bias_new,
      *([cache_k] * n_pages), *([cache_v] * n_pages))


def _mlstm_kernel(qk_ref, v_ref, og_ref, z_ref, if_ref, cw_ref, cb_ref, bif_ref, nw_ref,
                  c0_ref, n0_ref, m0_ref, buf0_ref,
                  om_ref, c_ref, n_ref, m_ref, ext_scr, m_scr, *, heads, taps):
    ci = pl.program_id(1)
    nc = pl.num_programs(1)
    L = qk_ref.shape[1]
    mw = v_ref.shape[2]
    dh = mw // heads
    pad = 8

    @pl.when(ci == 0)
    def _():
        c_ref[...] = c0_ref[...]
        n_ref[...] = n0_ref[...]
        for h in range(heads):
            m_scr[h] = jnp.broadcast_to(m0_ref[0, h:h + 1, :], m_scr.shape[1:])
        ext_scr[pad - (taps - 1):pad, :] = buf0_ref[0]

    ext_scr[pad:pad + L, :] = qk_ref[0]
    conv = cb_ref[...]
    for j in range(taps):
        r0 = pad - (taps - 1) + j
        conv = conv + ext_scr[r0:r0 + L, :] * cw_ref[j:j + 1, :]
    ext_scr[0:pad, :] = ext_scr[L:L + pad, :]
    qk = _silu(conv)
    q_all = qk[:, :mw]
    k_all = qk[:, mw:] * (dh ** -0.5)
    v_all = v_ref[0]

    g = if_ref[0] + bif_ref[...]
    lf_all = jnp.minimum(g, 0.0) - jnp.log(1.0 + jnp.exp(-jnp.abs(g)))

    row = lax.broadcasted_iota(jnp.int32, (L, L), 0)
    col = lax.broadcasted_iota(jnp.int32, (L, L), 1)
    eye = row == col
    low = col <= row

    for h in range(heads):
        li_col = g[:, h:h + 1]
        lf_col = lf_all[:, heads + h:heads + h + 1]
        lf_row = jnp.sum(jnp.where(eye, lf_col, 0.0), axis=0, keepdims=True)
        bt_col = jnp.sum(jnp.where(low, lf_row, 0.0), axis=1, keepdims=True)
        r_row = jnp.sum(jnp.where(eye, li_col - bt_col, 0.0), axis=0, keepdims=True)
        m_prev = m_scr[h, 0:1, 0:1]

        log_d = jnp.where(low, bt_col + r_row, -jnp.inf)
        log_inter = bt_col + m_prev
        m_t = jnp.maximum(log_inter, jnp.max(log_d, axis=1, keepdims=True))
        d_mat = jnp.exp(log_d - m_t)
        inter = jnp.exp(log_inter - m_t)

        sl = slice(h * dh, (h + 1) * dh)
        q_h, k_h = q_all[:, sl], k_all[:, sl]
        q_b, k_b, v_b = q_h.astype(BF16), k_h.astype(BF16), v_all[:, sl].astype(BF16)
        c_h = c_ref[0, h]
        n_h = n_ref[0, h:h + 1, :]

        s = lax.dot_general(q_b, k_b, (((1,), (1,)), ((), ())), preferred_element_type=F32) * d_mat
        num = (jnp.dot(s.astype(BF16), v_b, preferred_element_type=F32)
               + jnp.dot(q_b, c_h.astype(BF16), preferred_element_type=F32) * inter)
        den = jnp.sum(s, axis=1, keepdims=True) + inter * jnp.sum(q_h * n_h, axis=1, keepdims=True)
        den = jnp.maximum(jnp.abs(den), jnp.exp(-m_t))
        hh = num / den

        b_last = jnp.sum(lf_col, axis=0, keepdims=True)
        log_w = b_last - bt_col + li_col
        m_new = jnp.maximum(b_last + m_prev, jnp.max(log_w, axis=0, keepdims=True))
        wk = jnp.exp(log_w - m_new)
        decay = jnp.exp(b_last + m_prev - m_new)
        kw = k_h * wk
        upd = lax.dot_general(kw.astype(BF16), v_b, (((0,), (0,)), ((), ())), preferred_element_type=F32)
        c_ref[0, h] = decay * c_h + upd
        n_ref[0, h:h + 1, :] = decay * n_h + jnp.sum(kw, axis=0, keepdims=True)
        m_scr[h] = jnp.broadcast_to(m_new, m_scr.shape[1:])

        mu = jnp.mean(hh, axis=1, keepdims=True)
        xc = hh - mu
        var = jnp.mean(xc * xc, axis=1, keepdims=True)
        y = xc * lax.rsqrt(var + EPS) * nw_ref[:, sl] * _sigmoid(og_ref[0, :, sl])
        om_ref[0, :, sl] = (y * _silu(z_ref[0, :, sl])).astype(om_ref.dtype)

    @pl.when(ci == nc - 1)
    def _():
        for h in range(heads):
            m_ref[0, h:h + 1, :] = m_scr[h, 0:1, :]


def mlstm_branch(u, u_if, conv_w, conv_b, b_if, norm_w, c0, n0, m0, buf0,
                 *, chunk, qk_col, v_col, og_col, z_col):
    b, t, _ = u.shape
    _, heads, dh, _ = c0.shape
    mw = heads * dh
    taps = conv_w.shape[0]
    nc = t // chunk
    bif_pad = jnp.zeros((1, LANES), F32).at[0, :2 * heads].set(b_if)
    m0_pad = jnp.broadcast_to(m0[:, :, None], (b, heads, LANES))
    kern = functools.partial(_mlstm_kernel, heads=heads, taps=taps)
    const2 = lambda bi, ci: (0, 0)
    o_m, c, n, m = pl.pallas_call(
        kern,
        grid=(b, nc),
        in_specs=[
            pl.BlockSpec((1, chunk, 2 * mw), lambda bi, ci: (bi, ci, qk_col // (2 * mw))),
            pl.BlockSpec((1, chunk, mw), lambda bi, ci: (bi, ci, v_col // mw)),
            pl.BlockSpec((1, chunk, mw), lambda bi, ci: (bi, ci, og_col // mw)),
            pl.BlockSpec((1, chunk, mw), lambda bi, ci: (bi, ci, z_col // mw)),
            pl.BlockSpec((1, chunk, LANES), lambda bi, ci: (bi, ci, 0)),
            pl.BlockSpec((taps, 2 * mw), const2),
            pl.BlockSpec((1, 2 * mw), const2),
            pl.BlockSpec((1, LANES), const2),
            pl.BlockSpec((1, mw), const2),
            pl.BlockSpec((1, heads, dh, dh), lambda bi, ci: (bi, 0, 0, 0)),
            pl.BlockSpec((1, heads, dh), lambda bi, ci: (bi, 0, 0)),
            pl.BlockSpec((1, heads, LANES), lambda bi, ci: (bi, 0, 0)),
            pl.BlockSpec((1, taps - 1, 2 * mw), lambda bi, ci: (bi, 0, 0)),
        ],
        out_specs=[
            pl.BlockSpec((1, chunk, mw), lambda bi, ci: (bi, ci, 0)),
            pl.BlockSpec((1, heads, dh, dh), lambda bi, ci: (bi, 0, 0, 0)),
            pl.BlockSpec((1, heads, dh), lambda bi, ci: (bi, 0, 0)),
            pl.BlockSpec((1, heads, LANES), lambda bi, ci: (bi, 0, 0)),
        ],
        out_shape=[
            jax.ShapeDtypeStruct((b, t, mw), BF16),
            jax.ShapeDtypeStruct((b, heads, dh, dh), F32),
            jax.ShapeDtypeStruct((b, heads, dh), F32),
            jax.ShapeDtypeStruct((b, heads, LANES), F32),
        ],
        scratch_shapes=[pltpu.VMEM((chunk + 8, 2 * mw), F32), pltpu.VMEM((heads, 8, LANES), F32)],
        compiler_params=_params(("parallel", "arbitrary")),
        name="mlstm_branch",
    )(u, u, u, u, u_if, conv_w, conv_b.reshape(1, 2 * mw), bif_pad, norm_w.reshape(1, mw), c0, n0, m0_pad, buf0)
    return o_m, c, n, m[:, :, 0]


def _layer_group(x, lw, lam, lam_init, attend, c0, n0, m0, buf0, *, chunk, tm):
    b, t, d = x.shape
    heads, dqk = lw["heads"], lw["dqk"]
    aw, mw = lw["aw"], lw["mw"]
    col = lw["col"]
    x2 = x.reshape(b * t, d)
    u2, uif2 = in_projection(x2, lw["norm_pre"], lw["w_main"], lw["w_if"], tm=tm, tn=1024)
    u = u2.reshape(b, t, -1)
    k_rows = u[:, :, col["ak"]:col["ak"] + aw].reshape(b, t, heads, 2 * dqk)
    v_rows = u[:, :, col["av"]:col["av"] + aw].reshape(b, t, heads, 2 * dqk)
    scal = jnp.reshape(lam, (1,)).astype(F32)
    o_a = attend(u, k_rows, v_rows, scal)
    o_m, c, n, m = mlstm_branch(u, uif2.reshape(b, t, LANES), lw["conv_w"], lw["conv_b"], lw["b_if"],
                                lw["mlstm_norm_w"], c0, n0, m0, buf0, chunk=chunk,
                                qk_col=col["mq"], v_col=col["mv"], og_col=col["mo"], z_col=col["mz"])
    y = gated_merge(o_a.reshape(b * t, aw), o_m.reshape(b * t, mw), lw["w_pa"], lw["w_pm"], u2,
                    col["ga"], col["gm"], tm=tm, tn=1024)
    x_new = out_projection(y, lw["w_out"], lw["norm_post"], x2, tm=min(tm, 512)).reshape(b, t, d)
    taps = lw["conv_w"].shape[0]
    tail = u[:, t - (taps - 1):, col["mq"]:col["mq"] + 2 * mw]
    return x_new, k_rows, v_rows, c, n, m, tail


def kernel(x_prompt, x_sample, cache_k, cache_v, state_C, state_n, state_m, state_conv, page_table,
           norm_pre, norm_post, w_in, b_if, conv_w, conv_b, lambda_qk, attn_norm_w, mlstm_norm_w,
           w_pa, w_pm, w_out):
    depth = w_in.shape[0]
    d_model = x_prompt.shape[-1]
    heads = cache_k.shape[3]
    dqk = lambda_qk.shape[-1]
    aw = heads * 2 * dqk
    m_heads, m_dh = state_C.shape[2], state_C.shape[3]
    mw = m_heads * m_dh
    bp, sp, _ = x_prompt.shape
    n_if = 2 * m_heads
    if_col = 4 * aw + 4 * mw
    col = {"aq": 0, "ak": aw, "av": 2 * aw, "az": 3 * aw, "mq": 4 * aw, "mv": 4 * aw + 2 * mw,
           "mo": 4 * aw + 3 * mw, "mz": if_col, "ga": if_col + mw, "gm": if_col + mw + d_model}

    xp, xs = x_prompt, x_sample
    outs = [[] for _ in range(12)]
    for l in range(depth):
        lam_init = 0.8 - 0.6 * math.exp(-0.3 * l)
        lq = lambda_qk[l].astype(F32)
        lam = jnp.exp(jnp.sum(lq[0] * lq[1])) - jnp.exp(jnp.sum(lq[2] * lq[3])) + lam_init
        w_l = w_in[l]
        w_main = jnp.concatenate([w_l[:, :if_col], w_l[:, if_col + n_if:]], axis=1).astype(BF16)
        w_if = jnp.pad(w_l[:, if_col:if_col + n_if], ((0, 0), (0, LANES - n_if))).astype(BF16)
        lw = dict(heads=heads, dqk=dqk, aw=aw, mw=mw, col=col, norm_pre=norm_pre[l], norm_post=norm_post[l],
                  w_main=w_main, w_if=w_if, conv_w=conv_w[l], conv_b=conv_b[l], b_if=b_if[l],
                  mlstm_norm_w=mlstm_norm_w[l], w_pa=w_pa[l].astype(BF16), w_pm=w_pm[l].astype(BF16),
                  w_out=w_out[l].astype(BF16))
        post = 1.0 - lam_init

        def attend_prompt(u, k_rows, v_rows, scal):
            return prompt_attention(u, scal, attn_norm_w[l], heads=heads, dqk=dqk, q_col=col["aq"],
                                    k_col=col["ak"], v_col=col["av"], z_col=col["az"], post_scale=post, tq=512)

        def attend_sample(u, k_rows, v_rows, scal):
            return sample_attention(u, k_rows, v_rows, cache_k, cache_v, l, page_table, scal, attn_norm_w[l],
                                    dqk=dqk, q_col=col["aq"], z_col=col["az"], post_scale=post)

        zc = jnp.zeros((bp, m_heads, m_dh, m_dh), F32)
        zn = jnp.zeros((bp, m_heads, m_dh), F32)
        zm = jnp.zeros((bp, m_heads), F32)
        zb = jnp.zeros((bp, conv_w.shape[1] - 1, 2 * mw), F32)
        res_p = _layer_group(xp, lw, lam, lam_init, attend_prompt, zc, zn, zm, zb,
                             chunk=CHUNK if sp % CHUNK == 0 else sp, tm=1024)
        ts = xs.shape[1]
        res_s = _layer_group(xs, lw, lam, lam_init, attend_sample, state_C[l], state_n[l], state_m[l],
                             state_conv[l], chunk=CHUNK if ts % CHUNK == 0 else ts, tm=xs.shape[0] * ts)
        xp, xs = res_p[0], res_s[0]
        for i in range(6):
            outs[i].append(res_p[1 + i])
            outs[6 + i].append(res_s[1 + i])
    return (xp, xs) + tuple(jnp.stack(o) for o in outs)
```

```python
import functools
import math

import jax
import jax.numpy as jnp
import numpy as np
from jax import lax
from jax.experimental import pallas as pl
from jax.experimental.pallas import tpu as pltpu

F32 = jnp.float32
BF16 = jnp.bfloat16

EPS = 1e-6
CHUNK = 64
LANES = 128
SUBLANES = 8
BF16_ROWS = 16
NEG_BIG = -1e30
VMEM_LIMIT = 56 * 1024 * 1024
LOG2E = math.log2(math.e)


def _sigmoid(x):
    return 1.0 / (1.0 + jnp.exp(-x))


def _silu(x):
    return x * _sigmoid(x)


def _params(sem):
    return pltpu.CompilerParams(dimension_semantics=sem, vmem_limit_bytes=VMEM_LIMIT)


def _slabs(prev, layer, depth):
    return (depth, 0) if prev is None else (1, layer)


def _stacked_call(kernel_fn, *, grid, in_specs, inputs, out_specs, out_shape, n_stacked, prev, **kw):
    aliases = {}
    if prev is not None:
        first = len(out_shape) - n_stacked
        for k, p in enumerate(prev):
            aliases[len(inputs) + k] = first + k
        in_specs = list(in_specs) + [pl.BlockSpec(memory_space=pl.ANY)] * len(prev)
        inputs = list(inputs) + list(prev)
    return pl.pallas_call(kernel_fn, grid=grid, in_specs=in_specs, out_specs=out_specs, out_shape=out_shape,
                          input_output_aliases=aliases, **kw)(*inputs)


def _dot_nt(a, b):
    return lax.dot_general(a, b, (((1,), (1,)), ((), ())), preferred_element_type=F32)


def _inproj_kernel(x_ref, nw_ref, wh_ref, wt_ref, wif_ref, *rest, n_prev, n_head_blk, k_blk, v_blk, heads):
    u_ref, uif_ref, k_ref, v_ref, h_scr = rest[n_prev:]
    j = pl.program_id(1)
    tm = x_ref.shape[0]
    dh = k_ref.shape[-1]

    @pl.when(j == 0)
    def _():
        x = x_ref[...]
        ms = jnp.mean(x * x, axis=-1, keepdims=True)
        h = (x * lax.rsqrt(ms + EPS) * nw_ref[...]).astype(BF16)
        h_scr[...] = h
        uif_ref[...] = _dot_nt(h, wif_ref[...])

    @pl.when(j < n_head_blk)
    def _():
        acc = _dot_nt(h_scr[...], wh_ref[...])
        u_ref[...] = acc
        for blk, ref in ((k_blk, k_ref), (v_blk, v_ref)):
            @pl.when(j == blk)
            def _():
                for s in range(ref.shape[0]):
                    for h in range(heads):
                        ref[s, pl.ds(h, tm, stride=heads), :] = acc[:, h * dh:(h + 1) * dh]

    @pl.when(j >= n_head_blk)
    def _():
        u_ref[...] = _dot_nt(h_scr[...], wt_ref[...])


def in_projection(x, norm_w, w_head, w_tail, w_if, *, layer, depth, heads, k_col, v_col, prev_kv, tm, tn):
    t, d = x.shape
    nh, nt = w_head.shape[0] // tn, w_tail.shape[0] // tn
    n = (nh + nt) * tn
    dh = tn // heads
    assert k_col % tn == 0 and v_col % tn == 0 and tn == heads * dh
    kern = functools.partial(_inproj_kernel, n_prev=0 if prev_kv is None else 2, n_head_blk=nh,
                             k_blk=k_col // tn, v_blk=v_col // tn, heads=heads)
    nslab, slab = _slabs(prev_kv, layer, depth)
    kv_spec = pl.BlockSpec((nslab, tm * heads, dh), lambda i, j: (slab, i, 0))
    kv_shape = jax.ShapeDtypeStruct((depth, t * heads, dh), F32)
    return _stacked_call(
        kern,
        grid=(t // tm, nh + nt),
        in_specs=[
            pl.BlockSpec((tm, d), lambda i, j: (i, 0)),
            pl.BlockSpec((1, d), lambda i, j: (0, 0)),
            pl.BlockSpec((tn, d), lambda i, j: (jnp.minimum(j, nh - 1), 0)),
            pl.BlockSpec((tn, d), lambda i, j: (jnp.maximum(j - nh, 0), 0)),
            pl.BlockSpec((LANES, d), lambda i, j: (0, 0)),
        ],
        inputs=[x, norm_w.reshape(1, d), w_head, w_tail, w_if],
        out_specs=[
            pl.BlockSpec((tm, tn), lambda i, j: (i, j)),
            pl.BlockSpec((tm, LANES), lambda i, j: (i, 0)),
            kv_spec, kv_spec,
        ],
        out_shape=[jax.ShapeDtypeStruct((t, n), F32), jax.ShapeDtypeStruct((t, LANES), F32), kv_shape, kv_shape],
        n_stacked=2, prev=prev_kv,
        scratch_shapes=[pltpu.VMEM((tm, d), BF16)],
        compiler_params=_params(("parallel", "arbitrary")),
        name="in_projection",
    )


def _merge_kernel(oa_ref, om_ref, wpa_ref, wpm_ref, ga_ref, gm_ref, y_ref):
    pa = jnp.dot(oa_ref[...], wpa_ref[...], preferred_element_type=F32)
    pm = jnp.dot(om_ref[...], wpm_ref[...], preferred_element_type=F32)
    y = _sigmoid(ga_ref[...]) * pa + _sigmoid(gm_ref[...]) * pm
    y_ref[...] = y.astype(y_ref.dtype)


def gated_merge(o_a, o_m, w_pa, w_pm, u, ga_col, gm_col, *, tm, tn):
    t, wa = o_a.shape
    wm = o_m.shape[1]
    d = w_pa.shape[1]
    ga_blk, gm_blk = ga_col // tn, gm_col // tn
    return pl.pallas_call(
        _merge_kernel,
        grid=(t // tm, d // tn),
        in_specs=[
            pl.BlockSpec((tm, wa), lambda i, j: (i, 0)),
            pl.BlockSpec((tm, wm), lambda i, j: (i, 0)),
            pl.BlockSpec((wa, tn), lambda i, j: (0, j)),
            pl.BlockSpec((wm, tn), lambda i, j: (0, j)),
            pl.BlockSpec((tm, tn), lambda i, j: (i, ga_blk + j)),
            pl.BlockSpec((tm, tn), lambda i, j: (i, gm_blk + j)),
        ],
        out_specs=pl.BlockSpec((tm, tn), lambda i, j: (i, j)),
        out_shape=jax.ShapeDtypeStruct((t, d), BF16),
        compiler_params=_params(("parallel", "arbitrary")),
        name="gated_merge",
    )(o_a, o_m, w_pa, w_pm, u, u)


def _outproj_kernel(y_ref, w_ref, nw_ref, x_ref, o_ref):
    out = jnp.dot(y_ref[...], w_ref[...], preferred_element_type=F32)
    ms = jnp.mean(out * out, axis=-1, keepdims=True)
    o_ref[...] = x_ref[...] + out * lax.rsqrt(ms + EPS) * nw_ref[...]


def out_projection(y, w_out, norm_w, x, *, tm):
    t, d = x.shape
    return pl.pallas_call(
        _outproj_kernel,
        grid=(t // tm,),
        in_specs=[
            pl.BlockSpec((tm, d), lambda i: (i, 0)),
            pl.BlockSpec((d, d), lambda i: (0, 0)),
            pl.BlockSpec((1, d), lambda i: (0, 0)),
            pl.BlockSpec((tm, d), lambda i: (i, 0)),
        ],
        out_specs=pl.BlockSpec((tm, d), lambda i: (i, 0)),
        out_shape=jax.ShapeDtypeStruct((t, d), F32),
        compiler_params=_params(("parallel",)),
        name="out_projection",
    )(y, w_out, norm_w.reshape(1, d), x)


def _prompt_attn_kernel(scal_ref, q_ref, k_ref, v_ref, z_ref, nw_ref, o_ref, kb_scr, vt_scr,
                        *, tq, dqk, scale_log2, post_scale):
    qi = pl.program_id(2)
    lam = scal_ref[0]
    dv = v_ref.shape[-1]
    n_blk = vt_scr.shape[0]

    @pl.when(qi == 0)
    def _():
        kb_scr[...] = k_ref[0].astype(BF16)
        for c in range(n_blk):
            vt_scr[c, 0:dv, :] = jnp.transpose(v_ref[0, c * tq:(c + 1) * tq, :]).astype(BF16)
            vt_scr[c, dv:, :] = jnp.ones((BF16_ROWS, tq), BF16)

    q = q_ref[0] * scale_log2
    lane = lax.broadcasted_iota(jnp.int32, q.shape, 1)
    q1 = jnp.where(lane < dqk, q, 0.0).astype(BF16)
    q2 = jnp.where(lane >= dqk, q, 0.0).astype(BF16)

    def scores_t(k, qm):
        return lax.dot_general(k, qm, (((1,), (1,)), ((), ())), preferred_element_type=F32)

    def online(state, s_t, vt):
        m, acc = state
        m_new = jnp.maximum(m, jnp.max(s_t, axis=0, keepdims=True))
        alpha = jnp.exp2(m - m_new)
        p = jnp.exp2(s_t - m_new)
        acc = alpha * acc + jnp.dot(vt, p.astype(BF16), preferred_element_type=F32)
        return m_new, acc

    def scores_of(j):
        k = kb_scr[pl.ds(pl.multiple_of(j * tq, tq), tq), :]
        return scores_t(k, q1), scores_t(k, q2)

    def body(j, carry):
        st1, st2 = carry
        s1, s2 = scores_of(j)
        vt = vt_scr[j]
        return online(st1, s1, vt), online(st2, s2, vt)

    def init():
        return jnp.full((1, tq), NEG_BIG, F32), jnp.zeros((dv + BF16_ROWS, tq), F32)

    st1, st2 = lax.fori_loop(0, qi, body, (init(), init()))
    s1, s2 = scores_of(qi)
    krow = lax.broadcasted_iota(jnp.int32, s1.shape, 0)
    qcol = lax.broadcasted_iota(jnp.int32, s1.shape, 1)
    keep = krow <= qcol
    vt = vt_scr[qi]
    _, a1 = online(st1, jnp.where(keep, s1, -jnp.inf), vt)
    _, a2 = online(st2, jnp.where(keep, s2, -jnp.inf), vt)

    out_t = a1[0:dv] / a1[dv:dv + 1] - lam * (a2[0:dv] / a2[dv:dv + 1])
    out = jnp.transpose(out_t)
    ms = jnp.mean(out * out, axis=-1, keepdims=True)
    y = out * lax.rsqrt(ms + EPS) * nw_ref[...] * post_scale
    o_ref[0] = (y * _silu(z_ref[0])).astype(o_ref.dtype)


def prompt_attention(u, scal, norm_w, *, heads, dqk, q_col, k_col, v_col, z_col, post_scale, tq):
    b, s, _ = u.shape
    dh = 2 * dqk
    qb, kb, vb, zb = q_col // dh, k_col // dh, v_col // dh, z_col // dh
    kern = functools.partial(_prompt_attn_kernel, tq=tq, dqk=dqk, scale_log2=dqk ** -0.5 * LOG2E,
                             post_scale=post_scale)
    return pl.pallas_call(
        kern,
        grid=(b, heads, s // tq),
        in_specs=[
            pl.BlockSpec(memory_space=pltpu.SMEM),
            pl.BlockSpec((1, tq, dh), lambda bi, h, i: (bi, i, qb + h)),
            pl.BlockSpec((1, s, dh), lambda bi, h, i: (bi, 0, kb + h)),
            pl.BlockSpec((1, s, dh), lambda bi, h, i: (bi, 0, vb + h)),
            pl.BlockSpec((1, tq, dh), lambda bi, h, i: (bi, i, zb + h)),
            pl.BlockSpec((1, dh), lambda bi, h, i: (0, h)),
        ],
        out_specs=pl.BlockSpec((1, tq, dh), lambda bi, h, i: (bi, i, h)),
        out_shape=jax.ShapeDtypeStruct((b, s, heads * dh), BF16),
        scratch_shapes=[pltpu.VMEM((s, dh), BF16), pltpu.VMEM((s // tq, dh + BF16_ROWS, tq), BF16)],
        compiler_params=_params(("parallel", "parallel", "arbitrary")),
        name="prompt_attention",
    )(scal, u, u, u, u, norm_w.reshape(1, heads * dh))


def _sample_attn_kernel(pt_ref, scal_ref, q_ref, kn_ref, vn_ref, z_ref, nw_ref, bias_ref, biasn_ref, *rest,
                        n_pages, heads, dqk, nq, scale, post_scale):
    del pt_ref
    k_refs = rest[:n_pages]
    v_refs = rest[n_pages:2 * n_pages]
    o_ref = rest[2 * n_pages]
    s_scr = rest[2 * n_pages + 1]
    lam = scal_ref[0]
    dh = 2 * dqk
    rows_pp = k_refs[0].shape[0] * k_refs[0].shape[1]

    q = q_ref[0] * scale
    lane = lax.broadcasted_iota(jnp.int32, (nq, dh), 1)
    pieces = []
    for h in range(heads):
        qh = q[:, h * dh:(h + 1) * dh]
        pieces.append(jnp.where(lane < dqk, qh, 0.0))
        pieces.append(jnp.where(lane >= dqk, qh, 0.0))
    qall = jnp.concatenate(pieces, axis=0).astype(BF16)
    nrow = qall.shape[0]

    def scores(k2d):
        return lax.dot_general(qall, k2d, (((1,), (1,)), ((), ())), preferred_element_type=F32)

    bias = bias_ref[...]
    m = jnp.full((nrow, 1), NEG_BIG, F32)
    for j in range(n_pages):
        kj = k_refs[j][...].reshape(rows_pp, dh).astype(BF16)
        s = scores(kj) + bias
        s_scr[:, j * rows_pp:(j + 1) * rows_pp] = s
        m = jnp.maximum(m, jnp.max(s, axis=-1, keepdims=True))
    sn = scores(kn_ref[...].astype(BF16)) + biasn_ref[...]
    m = jnp.maximum(m, jnp.max(sn, axis=-1, keepdims=True))

    l = jnp.zeros((nrow, 1), F32)
    acc = jnp.zeros((nrow, dh), F32)
    for j in range(n_pages):
        p = jnp.exp(s_scr[:, j * rows_pp:(j + 1) * rows_pp] - m)
        l = l + jnp.sum(p, axis=-1, keepdims=True)
        vj = v_refs[j][...].reshape(rows_pp, dh).astype(BF16)
        acc = acc + jnp.dot(p.astype(BF16), vj, preferred_element_type=F32)
    pn = jnp.exp(sn - m)
    l = l + jnp.sum(pn, axis=-1, keepdims=True)
    acc = acc + jnp.dot(pn.astype(BF16), vn_ref[...].astype(BF16), preferred_element_type=F32)

    o = acc / l
    z = z_ref[0]
    nw = nw_ref[...]
    for h in range(heads):
        r0 = h * 2 * nq
        oh = o[r0:r0 + nq] - lam * o[r0 + nq:r0 + 2 * nq]
        ms = jnp.mean(oh * oh, axis=-1, keepdims=True)
        y = oh * lax.rsqrt(ms + EPS) * nw[:, h * dh:(h + 1) * dh] * post_scale
        o_ref[0, :, h * dh:(h + 1) * dh] = (y * _silu(z[:, h * dh:(h + 1) * dh])).astype(o_ref.dtype)


def _sample_masks(heads, nq, page_rows):
    r = np.arange(heads * 2 * nq)
    rh, rt = r // (2 * nq), r % nq
    c = np.arange(page_rows)
    past = np.where((c[None, :] % heads) == rh[:, None], 0.0, -np.inf).astype(np.float32)
    cn = np.arange(nq * heads)
    ok = ((cn[None, :] % heads) == rh[:, None]) & ((cn[None, :] // heads) <= rt[:, None])
    new = np.where(ok, 0.0, -np.inf).astype(np.float32)
    return jnp.asarray(past), jnp.asarray(new)


def sample_attention(u, k_rows, v_rows, kv_layer, cache_k, cache_v, layer, page_table, scal, norm_w,
                     *, dqk, q_col, z_col, post_scale):
    b, nq, _ = u.shape
    _, _, page, heads, dh = cache_k.shape
    n_pages = page_table.shape[1]
    width = heads * dh
    qb, zb = q_col // width, z_col // width
    bias, bias_new = _sample_masks(heads, nq, page * heads)
    nrow = heads * 2 * nq

    def page_spec(j):
        return pl.BlockSpec((None, None, page, heads, dh), lambda i, pt: (layer, pt[i, j], 0, 0, 0))

    new_spec = pl.BlockSpec((None, nq * heads, dh), lambda i, pt: (kv_layer, i, 0))
    kern = functools.partial(_sample_attn_kernel, n_pages=n_pages, heads=heads, dqk=dqk, nq=nq,
                             scale=dqk ** -0.5, post_scale=post_scale)
    grid_spec = pltpu.PrefetchScalarGridSpec(
        num_scalar_prefetch=1,
        grid=(b,),
        in_specs=[
            pl.BlockSpec(memory_space=pltpu.SMEM),
            pl.BlockSpec((1, nq, width), lambda i, pt: (i, 0, qb)),
            new_spec, new_spec,
            pl.BlockSpec((1, nq, width), lambda i, pt: (i, 0, zb)),
            pl.BlockSpec((1, width), lambda i, pt: (0, 0)),
            pl.BlockSpec(bias.shape, lambda i, pt: (0, 0)),
            pl.BlockSpec(bias_new.shape, lambda i, pt: (0, 0)),
        ] + [page_spec(j) for j in range(n_pages)] * 2,
        out_specs=pl.BlockSpec((1, nq, width), lambda i, pt: (i, 0, 0)),
        scratch_shapes=[pltpu.VMEM((nrow, n_pages * page * heads), F32)],
    )
    return pl.pallas_call(
        kern,
        grid_spec=grid_spec,
        out_shape=jax.ShapeDtypeStruct((b, nq, width), BF16),
        compiler_params=_params(("arbitrary",)),
        name="sample_attention",
    )(page_table, scal, u, k_rows, v_rows, u, norm_w.reshape(1, width), bias, bias_new,
      *([cache_k] * n_pages), *([cache_v] * n_pages))


def _mlstm_kernel(qk_ref, v_ref, og_ref, z_ref, if_ref, cw_ref, cb_ref, bif_ref, nw_ref,
                  c0_ref, n0_ref, m0_ref, buf0_ref, *rest, n_prev, heads, taps):
    om_ref, c_ref, n_ref, m_ref, tail_ref, ext_scr, m_scr = rest[n_prev:]
    ci = pl.program_id(1)
    nc = pl.num_programs(1)
    nb, L = qk_ref.shape[0], qk_ref.shape[1]
    mw = v_ref.shape[2]
    dh = mw // heads
    pad = SUBLANES

    @pl.when(ci == 0)
    def _():
        c_ref[0] = c0_ref[...]
        n_ref[0] = n0_ref[...]
        for bb in range(nb):
            for h in range(heads):
                m_scr[bb * heads + h] = jnp.broadcast_to(m0_ref[bb, h:h + 1, :], m_scr.shape[1:])
            ext_scr[bb, pad - (taps - 1):pad, :] = buf0_ref[bb]

    row = lax.broadcasted_iota(jnp.int32, (L, L), 0)
    col = lax.broadcasted_iota(jnp.int32, (L, L), 1)
    eye = row == col
    low = col <= row

    for bb in range(nb):
        ext_scr[bb, pad:pad + L, :] = qk_ref[bb]
        conv = cb_ref[...]
        for j in range(taps):
            r0 = pad - (taps - 1) + j
            conv = conv + ext_scr[bb, r0:r0 + L, :] * cw_ref[j:j + 1, :]

        ext_scr[bb, 0:pad, :] = ext_scr[bb, L:L + pad, :]
        qk = _silu(conv)
        q_all = qk[:, :mw]
        k_all = qk[:, mw:] * (dh ** -0.5)
        v_all = v_ref[bb]

        g = if_ref[bb] + bif_ref[...]
        lf_all = jnp.minimum(g, 0.0) - jnp.log(1.0 + jnp.exp(-jnp.abs(g)))

        for h in range(heads):
            li_col = g[:, h:h + 1]
            lf_col = lf_all[:, heads + h:heads + h + 1]
            lf_row = jnp.sum(jnp.where(eye, lf_col, 0.0), axis=0, keepdims=True)
            bt_col = jnp.sum(jnp.where(low, lf_row, 0.0), axis=1, keepdims=True)
            r_row = jnp.sum(jnp.where(eye, li_col - bt_col, 0.0), axis=0, keepdims=True)
            m_prev = m_scr[bb * heads + h, 0:1, 0:1]

            log_d = jnp.where(low, bt_col + r_row, -jnp.inf)
            log_inter = bt_col + m_prev
            m_t = jnp.maximum(log_inter, jnp.max(log_d, axis=1, keepdims=True))
            d_mat = jnp.exp(log_d - m_t)
            inter = jnp.exp(log_inter - m_t)

            sl = slice(h * dh, (h + 1) * dh)
            q_h, k_h = q_all[:, sl], k_all[:, sl]
            q_b, k_b, v_b = q_h.astype(BF16), k_h.astype(BF16), v_all[:, sl].astype(BF16)
            c_h = c_ref[0, bb, h]
            n_h = n_ref[0, bb, h:h + 1, :]

            s = lax.dot_general(q_b, k_b, (((1,), (1,)), ((), ())), preferred_element_type=F32) * d_mat
            num = (jnp.dot(s.astype(BF16), v_b, preferred_element_type=F32)
                   + jnp.dot(q_b, c_h.astype(BF16), preferred_element_type=F32) * inter)
            den = jnp.sum(s, axis=1, keepdims=True) + inter * jnp.sum(q_h * n_h, axis=1, keepdims=True)
            den = jnp.maximum(jnp.abs(den), jnp.exp(-m_t))
            hh = num / den

            b_last = jnp.sum(lf_col, axis=0, keepdims=True)
            log_w = b_last - bt_col + li_col
            m_new = jnp.maximum(b_last + m_prev, jnp.max(log_w, axis=0, keepdims=True))
            wk = jnp.exp(log_w - m_new)
            decay = jnp.exp(b_last + m_prev - m_new)
            kw = k_h * wk
            upd = lax.dot_general(kw.astype(BF16), v_b, (((0,), (0,)), ((), ())), preferred_element_type=F32)
            c_ref[0, bb, h] = decay * c_h + upd
            n_ref[0, bb, h:h + 1, :] = decay * n_h + jnp.sum(kw, axis=0, keepdims=True)
            m_scr[bb * heads + h] = jnp.broadcast_to(m_new, m_scr.shape[1:])

            mu = jnp.mean(hh, axis=1, keepdims=True)
            xc = hh - mu
            var = jnp.mean(xc * xc, axis=1, keepdims=True)
            y = xc * lax.rsqrt(var + EPS) * nw_ref[:, sl] * _sigmoid(og_ref[bb, :, sl])
            om_ref[bb, :, sl] = (y * _silu(z_ref[bb, :, sl])).astype(om_ref.dtype)

    @pl.when(ci == nc - 1)
    def _():
        for s in range(c_ref.shape[0]):
            if s > 0:
                c_ref[s] = c_ref[0]
                n_ref[s] = n_ref[0]
            for bb in range(nb):
                tail_ref[s, bb] = ext_scr[bb, pad - (taps - 1):pad, :]
                for h in range(heads):
                    m_ref[s, bb, h:h + 1, :] = m_scr[bb * heads + h, 0:1, :]


def mlstm_branch(u, u_if, conv_w, conv_b, b_if, norm_w, c0, n0, m0, buf0, *, state_layer, layer, depth, prev,
                 chunk, nb, qk_col, v_col, og_col, z_col):
    b, t, _ = u.shape
    _, _, heads, dh, _ = c0.shape
    mw = heads * dh
    taps = conv_w.shape[0]
    assert chunk >= taps - 1 and t % chunk == 0 and b % nb == 0
    bif_pad = jnp.zeros((1, LANES), F32).at[0, :2 * heads].set(b_if)
    kern = functools.partial(_mlstm_kernel, n_prev=0 if prev is None else 4, heads=heads, taps=taps)
    const2 = lambda bi, ci: (0, 0)
    st_in = lambda bi, ci: (state_layer, bi, 0, 0)
    nslab, slab = _slabs(prev, layer, depth)
    st_out = lambda bi, ci: (slab, bi, 0, 0)
    res = _stacked_call(
        kern,
        grid=(b // nb, t // chunk),
        in_specs=[
            pl.BlockSpec((nb, chunk, 2 * mw), lambda bi, ci: (bi, ci, qk_col // (2 * mw))),
            pl.BlockSpec((nb, chunk, mw), lambda bi, ci: (bi, ci, v_col // mw)),
            pl.BlockSpec((nb, chunk, mw), lambda bi, ci: (bi, ci, og_col // mw)),
            pl.BlockSpec((nb, chunk, mw), lambda bi, ci: (bi, ci, z_col // mw)),
            pl.BlockSpec((nb, chunk, LANES), lambda bi, ci: (bi, ci, 0)),
            pl.BlockSpec((taps, 2 * mw), const2),
            pl.BlockSpec((1, 2 * mw), const2),
            pl.BlockSpec((1, LANES), const2),
            pl.BlockSpec((1, mw), const2),
            pl.BlockSpec((None, nb, heads, dh, dh), lambda bi, ci: (state_layer, bi, 0, 0, 0)),
            pl.BlockSpec((None, nb, heads, dh), st_in),
            pl.BlockSpec((None, nb, heads, LANES), st_in),
            pl.BlockSpec((None, nb, taps - 1, 2 * mw), st_in),
        ],
        inputs=[u, u, u, u, u_if, conv_w, conv_b.reshape(1, 2 * mw), bif_pad, norm_w.reshape(1, mw),
                c0, n0, m0, buf0],
        out_specs=[
            pl.BlockSpec((nb, chunk, mw), lambda bi, ci: (bi, ci, 0)),
            pl.BlockSpec((nslab, nb, heads, dh, dh), lambda bi, ci: (slab, bi, 0, 0, 0)),
            pl.BlockSpec((nslab, nb, heads, dh), st_out),
            pl.BlockSpec((nslab, nb, heads, LANES), st_out),
            pl.BlockSpec((nslab, nb, taps - 1, 2 * mw), st_out),
        ],
        out_shape=[
            jax.ShapeDtypeStruct((b, t, mw), BF16),
            jax.ShapeDtypeStruct((depth, b, heads, dh, dh), F32),
            jax.ShapeDtypeStruct((depth, b, heads, dh), F32),
            jax.ShapeDtypeStruct((depth, b, heads, LANES), F32),
            jax.ShapeDtypeStruct((depth, b, taps - 1, 2 * mw), F32),
        ],
        n_stacked=4, prev=prev,
        scratch_shapes=[pltpu.VMEM((nb, chunk + SUBLANES, 2 * mw), F32),
                        pltpu.VMEM((nb * heads, SUBLANES, LANES), F32)],
        compiler_params=_params(("parallel", "arbitrary")),
        name="mlstm_branch",
    )
    return res[0], tuple(res[1:])


def _layer_group(x, lw, lam, attend, states, *, layer, depth, prev, chunk, nb, tm):
    b, t, d = x.shape
    heads, aw, mw, col = lw["heads"], lw["aw"], lw["mw"], lw["col"]
    x2 = x.reshape(b * t, d)
    u2, uif2, k_st, v_st = in_projection(
        x2, lw["norm_pre"], lw["w_head"], lw["w_tail"], lw["w_if"], layer=layer, depth=depth, heads=heads,
        k_col=col["ak"], v_col=col["av"], prev_kv=None if prev is None else prev[0], tm=tm, tn=aw)
    u = u2.reshape(b, t, -1)
    scal = jnp.reshape(lam, (1,)).astype(F32)
    o_a = attend(u, k_st, v_st, scal)
    c0, n0, m0, buf0, state_layer = states
    o_m, st = mlstm_branch(u, uif2.reshape(b, t, LANES), lw["conv_w"], lw["conv_b"], lw["b_if"],
                           lw["mlstm_norm_w"], c0, n0, m0, buf0, state_layer=state_layer, layer=layer,
                           depth=depth, prev=None if prev is None else prev[1], chunk=chunk, nb=nb,
                           qk_col=col["mq"], v_col=col["mv"], og_col=col["mo"], z_col=col["mz"])
    y = gated_merge(o_a.reshape(b * t, aw), o_m.reshape(b * t, mw), lw["w_pa"], lw["w_pm"], u2,
                    col["ga"], col["gm"], tm=tm, tn=1024)
    x_new = out_projection(y, lw["w_out"], lw["norm_post"], x2, tm=tm).reshape(b, t, d)
    return x_new, ((k_st, v_st), st)


def kernel(x_prompt, x_sample, cache_k, cache_v, state_C, state_n, state_m, state_conv, page_table,
           norm_pre, norm_post, w_in, b_if, conv_w, conv_b, lambda_qk, attn_norm_w, mlstm_norm_w,
           w_pa, w_pm, w_out):
    depth = w_in.shape[0]
    d_model = x_prompt.shape[-1]
    heads = cache_k.shape[3]
    dqk = lambda_qk.shape[-1]
    dh = 2 * dqk
    aw = heads * dh
    m_heads, m_dh = state_C.shape[2], state_C.shape[3]
    mw = m_heads * m_dh
    bp, sp, _ = x_prompt.shape
    bs, ts, _ = x_sample.shape
    taps = conv_w.shape[1]
    n_if = 2 * m_heads
    if_col = 4 * aw + 4 * mw
    col = {"aq": 0, "ak": aw, "av": 2 * aw, "az": 3 * aw, "mq": 4 * aw, "mv": 4 * aw + 2 * mw,
           "mo": 4 * aw + 3 * mw, "mz": if_col, "ga": if_col + mw, "gm": if_col + mw + d_model}

    zero_states = (jnp.zeros((1, bp, m_heads, m_dh, m_dh), F32), jnp.zeros((1, bp, m_heads, m_dh), F32),
                   jnp.zeros((1, bp, m_heads, LANES), F32), jnp.zeros((1, bp, taps - 1, 2 * mw), F32))
    m_rep = jnp.broadcast_to(state_m[..., None], state_m.shape + (LANES,))

    xp, xs = x_prompt, x_sample
    prev_p = prev_s = None
    for l in range(depth):
        lam_init = 0.8 - 0.6 * math.exp(-0.3 * l)
        lq = lambda_qk[l].astype(F32)
        lam = jnp.exp(jnp.sum(lq[0] * lq[1])) - jnp.exp(jnp.sum(lq[2] * lq[3])) + lam_init
        w_l = w_in[l]
        lw = dict(heads=heads, aw=aw, mw=mw, col=col, norm_pre=norm_pre[l], norm_post=norm_post[l],
                  w_head=jnp.swapaxes(w_l[:, :if_col], 0, 1).astype(BF16),
                  w_tail=jnp.swapaxes(w_l[:, if_col + n_if:], 0, 1).astype(BF16),
                  w_if=jnp.pad(jnp.swapaxes(w_l[:, if_col:if_col + n_if], 0, 1),
                               ((0, LANES - n_if), (0, 0))).astype(BF16),
                  conv_w=conv_w[l], conv_b=conv_b[l], b_if=b_if[l], mlstm_norm_w=mlstm_norm_w[l],
                  w_pa=w_pa[l].astype(BF16), w_pm=w_pm[l].astype(BF16), w_out=w_out[l].astype(BF16))
        post = 1.0 - lam_init

        def attend_prompt(u, k_st, v_st, scal):
            return prompt_attention(u, scal, attn_norm_w[l], heads=heads, dqk=dqk, q_col=col["aq"],
                                    k_col=col["ak"], v_col=col["av"], z_col=col["az"], post_scale=post, tq=512)

        def attend_sample(u, k_st, v_st, scal):
            return sample_attention(u, k_st, v_st, l, cache_k, cache_v, l, page_table, scal, attn_norm_w[l],
                                    dqk=dqk, q_col=col["aq"], z_col=col["az"], post_scale=post)

        xp, prev_p = _layer_group(xp, lw, lam, attend_prompt, zero_states + (0,), layer=l, depth=depth,
                                  prev=prev_p, chunk=CHUNK if sp % CHUNK == 0 else sp, nb=1, tm=512)
        xs, prev_s = _layer_group(xs, lw, lam, attend_sample, (state_C, state_n, m_rep, state_conv, l), layer=l,
                                  depth=depth, prev=prev_s, chunk=CHUNK if ts % CHUNK == 0 else ts, nb=4, tm=bs * ts)

    def unpack(prev, b, t):
        (k_st, v_st), (c, n, m, tail) = prev
        return (k_st.reshape(depth, b, t, heads, dh), v_st.reshape(depth, b, t, heads, dh), c, n, m[..., 0], tail)

    return (xp, xs) + unpack(prev_p, bp, sp) + unpack(prev_s, bs, ts)
```

```python
import functools
import math

import jax
import jax.numpy as jnp
import numpy as np
from jax import lax
from jax.experimental import pallas as pl
from jax.experimental.pallas import tpu as pltpu

F32 = jnp.float32
BF16 = jnp.bfloat16

EPS = 1e-6
CHUNK = 64
LANES = 128
SUBLANES = 8
BF16_ROWS = 16
NEG_BIG = -1e30
VMEM_LIMIT = 56 * 1024 * 1024
LOG2E = math.log2(math.e)


def _sigmoid(x):
    return 1.0 / (1.0 + jnp.exp(-x))


def _silu(x):
    return x * _sigmoid(x)


def _params(sem):
    return pltpu.CompilerParams(dimension_semantics=sem, vmem_limit_bytes=VMEM_LIMIT)


def _slabs(prev, layer, depth):
    return (depth, 0) if prev is None else (1, layer)


def _stacked_call(kernel_fn, *, grid, in_specs, inputs, out_specs, out_shape, n_stacked, prev, **kw):
    aliases = {}
    if prev is not None:
        first = len(out_shape) - n_stacked
        for k, p in enumerate(prev):
            aliases[len(inputs) + k] = first + k
        in_specs = list(in_specs) + [pl.BlockSpec(memory_space=pl.ANY)] * len(prev)
        inputs = list(inputs) + list(prev)
    return pl.pallas_call(kernel_fn, grid=grid, in_specs=in_specs, out_specs=out_specs, out_shape=out_shape,
                          input_output_aliases=aliases, **kw)(*inputs)


def _dot_nt(a, b):
    return lax.dot_general(a, b, (((1,), (1,)), ((), ())), preferred_element_type=F32)


def _inproj_kernel(x_ref, nw_ref, w_ref, wif_ref, *rest, n_prev, k_blk, v_blk, heads, nslab, slab):
    u_ref, uif_ref, k_hbm, v_hbm, h_scr, k_stage, v_stage, sems = rest[n_prev:]
    i, j = pl.program_id(0), pl.program_id(1)
    tm = x_ref.shape[0]
    dh = k_stage.shape[-1]
    rows = tm * heads
    row0 = pl.multiple_of(i * rows, rows)
    targets = ((k_blk, k_stage, k_hbm), (v_blk, v_stage, v_hbm))

    def copies(which):
        _, stage, hbm = targets[which]
        return [pltpu.make_async_copy(stage, hbm.at[slab + s, pl.ds(row0, rows), :], sems.at[which, s])
                for s in range(nslab)]

    @pl.when(j == 0)
    def _():
        x = x_ref[...]
        ms = jnp.mean(x * x, axis=-1, keepdims=True)
        h = (x * lax.rsqrt(ms + EPS) * nw_ref[...]).astype(BF16)
        h_scr[...] = h
        uif_ref[...] = _dot_nt(h, wif_ref[...])

    acc = _dot_nt(h_scr[...], w_ref[...])
    u_ref[...] = acc

    for which, (blk, stage, _) in enumerate(targets):
        @pl.when(j == blk)
        def _():
            for h in range(heads):
                stage[pl.ds(h, tm, stride=heads), :] = acc[:, h * dh:(h + 1) * dh]
            for c in copies(which):
                c.start()

    @pl.when(j == pl.num_programs(1) - 1)
    def _():
        for which in range(len(targets)):
            for c in copies(which):
                c.wait()


def in_projection(x, norm_w, w_t, w_if, *, layer, depth, heads, k_col, v_col, prev_kv, tm, tn):
    t, d = x.shape
    n = w_t.shape[0]
    dh = tn // heads
    k_blk, v_blk = k_col // tn, v_col // tn
    assert k_col % tn == 0 and v_col % tn == 0 and tn == heads * dh and max(k_blk, v_blk) < n // tn - 1
    nslab, slab = _slabs(prev_kv, layer, depth)
    kern = functools.partial(_inproj_kernel, n_prev=0 if prev_kv is None else 2, k_blk=k_blk, v_blk=v_blk,
                             heads=heads, nslab=nslab, slab=slab)
    kv_spec = pl.BlockSpec(memory_space=pl.ANY)
    kv_shape = jax.ShapeDtypeStruct((depth, t * heads, dh), F32)
    stage = pltpu.VMEM((tm * heads, dh), F32)
    return _stacked_call(
        kern,
        grid=(t // tm, n // tn),
        in_specs=[
            pl.BlockSpec((tm, d), lambda i, j: (i, 0)),
            pl.BlockSpec((1, d), lambda i, j: (0, 0)),
            pl.BlockSpec((tn, d), lambda i, j: (j, 0)),
            pl.BlockSpec((LANES, d), lambda i, j: (0, 0)),
        ],
        inputs=[x, norm_w.reshape(1, d), w_t, w_if],
        out_specs=[
            pl.BlockSpec((tm, tn), lambda i, j: (i, j)),
            pl.BlockSpec((tm, LANES), lambda i, j: (i, 0)),
            kv_spec, kv_spec,
        ],
        out_shape=[jax.ShapeDtypeStruct((t, n), F32), jax.ShapeDtypeStruct((t, LANES), F32), kv_shape, kv_shape],
        n_stacked=2, prev=prev_kv,
        scratch_shapes=[pltpu.VMEM((tm, d), BF16), stage, stage, pltpu.SemaphoreType.DMA((2, nslab))],
        compiler_params=_params(("arbitrary", "arbitrary")),
        name="in_projection",
    )


def _merge_kernel(oa_ref, om_ref, wpa_ref, wpm_ref, ga_ref, gm_ref, y_ref):
    pa = jnp.dot(oa_ref[...], wpa_ref[...], preferred_element_type=F32)
    pm = jnp.dot(om_ref[...], wpm_ref[...], preferred_element_type=F32)
    y = _sigmoid(ga_ref[...]) * pa + _sigmoid(gm_ref[...]) * pm
    y_ref[...] = y.astype(y_ref.dtype)


def gated_merge(o_a, o_m, w_pa, w_pm, u, ga_col, gm_col, *, tm, tn):
    t, wa = o_a.shape
    wm = o_m.shape[1]
    d = w_pa.shape[1]
    ga_blk, gm_blk = ga_col // tn, gm_col // tn
    return pl.pallas_call(
        _merge_kernel,
        grid=(t // tm, d // tn),
        in_specs=[
            pl.BlockSpec((tm, wa), lambda i, j: (i, 0)),
            pl.BlockSpec((tm, wm), lambda i, j: (i, 0)),
            pl.BlockSpec((wa, tn), lambda i, j: (0, j)),
            pl.BlockSpec((wm, tn), lambda i, j: (0, j)),
            pl.BlockSpec((tm, tn), lambda i, j: (i, ga_blk + j)),
            pl.BlockSpec((tm, tn), lambda i, j: (i, gm_blk + j)),
        ],
        out_specs=pl.BlockSpec((tm, tn), lambda i, j: (i, j)),
        out_shape=jax.ShapeDtypeStruct((t, d), BF16),
        compiler_params=_params(("parallel", "arbitrary")),
        name="gated_merge",
    )(o_a, o_m, w_pa, w_pm, u, u)


def _outproj_kernel(y_ref, w_ref, nw_ref, x_ref, o_ref):
    out = jnp.dot(y_ref[...], w_ref[...], preferred_element_type=F32)
    ms = jnp.mean(out * out, axis=-1, keepdims=True)
    o_ref[...] = x_ref[...] + out * lax.rsqrt(ms + EPS) * nw_ref[...]


def out_projection(y, w_out, norm_w, x, *, tm):
    t, d = x.shape
    return pl.pallas_call(
        _outproj_kernel,
        grid=(t // tm,),
        in_specs=[
            pl.BlockSpec((tm, d), lambda i: (i, 0)),
            pl.BlockSpec((d, d), lambda i: (0, 0)),
            pl.BlockSpec((1, d), lambda i: (0, 0)),
            pl.BlockSpec((tm, d), lambda i: (i, 0)),
        ],
        out_specs=pl.BlockSpec((tm, d), lambda i: (i, 0)),
        out_shape=jax.ShapeDtypeStruct((t, d), F32),
        compiler_params=_params(("parallel",)),
        name="out_projection",
    )(y, w_out, norm_w.reshape(1, d), x)


def _prompt_attn_kernel(scal_ref, q_ref, k_ref, v_ref, z_ref, nw_ref, o_ref, kb_scr, vt_scr,
                        *, tq, dqk, scale_log2, post_scale):
    qi = pl.program_id(2)
    lam = scal_ref[0]
    dv = v_ref.shape[-1]
    n_blk = vt_scr.shape[0]

    @pl.when(qi == 0)
    def _():
        kb_scr[...] = k_ref[0].astype(BF16)
        for c in range(n_blk):
            vt_scr[c, 0:dv, :] = jnp.transpose(v_ref[0, c * tq:(c + 1) * tq, :]).astype(BF16)
            vt_scr[c, dv:, :] = jnp.ones((BF16_ROWS, tq), BF16)

    q = q_ref[0] * scale_log2
    lane = lax.broadcasted_iota(jnp.int32, q.shape, 1)
    q1 = jnp.where(lane < dqk, q, 0.0).astype(BF16)
    q2 = jnp.where(lane >= dqk, q, 0.0).astype(BF16)

    def scores_t(k, qm):
        return lax.dot_general(k, qm, (((1,), (1,)), ((), ())), preferred_element_type=F32)

    def online(state, s_t, vt):
        m, acc = state
        m_new = jnp.maximum(m, jnp.max(s_t, axis=0, keepdims=True))
        alpha = jnp.exp2(m - m_new)
        p = jnp.exp2(s_t - m_new)
        acc = alpha * acc + jnp.dot(vt, p.astype(BF16), preferred_element_type=F32)
        return m_new, acc

    def scores_of(j):
        k = kb_scr[pl.ds(pl.multiple_of(j * tq, tq), tq), :]
        return scores_t(k, q1), scores_t(k, q2)

    def causal(s):
        krow = lax.broadcasted_iota(jnp.int32, s.shape, 0)
        qcol = lax.broadcasted_iota(jnp.int32, s.shape, 1)
        return jnp.where(krow <= qcol, s, -jnp.inf)

    def pair(j, carry, second_is_diagonal):
        st1, st2 = carry
        sa1, sa2 = scores_of(j)
        sb1, sb2 = scores_of(j + 1)
        if second_is_diagonal:
            sb1, sb2 = causal(sb1), causal(sb2)
        vta, vtb = vt_scr[j], vt_scr[j + 1]
        return online(online(st1, sa1, vta), sb1, vtb), online(online(st2, sa2, vta), sb2, vtb)

    def diagonal_only(carry):
        st1, st2 = carry
        s1, s2 = scores_of(qi)
        vt = vt_scr[qi]
        return online(st1, causal(s1), vt), online(st2, causal(s2), vt)

    def init():
        return jnp.full((1, tq), NEG_BIG, F32), jnp.zeros((dv + BF16_ROWS, tq), F32)

    carry = lax.fori_loop(0, qi // 2, lambda p, c: pair(2 * p, c, False), (init(), init()))
    (_, a1), (_, a2) = lax.cond(qi % 2 == 1, lambda c: pair(qi - 1, c, True), diagonal_only, carry)

    out_t = a1[0:dv] / a1[dv:dv + 1] - lam * (a2[0:dv] / a2[dv:dv + 1])
    out = jnp.transpose(out_t)
    ms = jnp.mean(out * out, axis=-1, keepdims=True)
    y = out * lax.rsqrt(ms + EPS) * nw_ref[...] * post_scale
    o_ref[0] = (y * _silu(z_ref[0])).astype(o_ref.dtype)


def prompt_attention(u, scal, norm_w, *, heads, dqk, q_col, k_col, v_col, z_col, post_scale, tq):
    b, s, _ = u.shape
    dh = 2 * dqk
    qb, kb, vb, zb = q_col // dh, k_col // dh, v_col // dh, z_col // dh
    kern = functools.partial(_prompt_attn_kernel, tq=tq, dqk=dqk, scale_log2=dqk ** -0.5 * LOG2E,
                             post_scale=post_scale)
    return pl.pallas_call(
        kern,
        grid=(b, heads, s // tq),
        in_specs=[
            pl.BlockSpec(memory_space=pltpu.SMEM),
            pl.BlockSpec((1, tq, dh), lambda bi, h, i: (bi, i, qb + h)),
            pl.BlockSpec((1, s, dh), lambda bi, h, i: (bi, 0, kb + h)),
            pl.BlockSpec((1, s, dh), lambda bi, h, i: (bi, 0, vb + h)),
            pl.BlockSpec((1, tq, dh), lambda bi, h, i: (bi, i, zb + h)),
            pl.BlockSpec((1, dh), lambda bi, h, i: (0, h)),
        ],
        out_specs=pl.BlockSpec((1, tq, dh), lambda bi, h, i: (bi, i, h)),
        out_shape=jax.ShapeDtypeStruct((b, s, heads * dh), BF16),
        scratch_shapes=[pltpu.VMEM((s, dh), BF16), pltpu.VMEM((s // tq, dh + BF16_ROWS, tq), BF16)],
        compiler_params=_params(("parallel", "parallel", "arbitrary")),
        name="prompt_attention",
    )(scal, u, u, u, u, norm_w.reshape(1, heads * dh))


def _sample_attn_kernel(pt_ref, scal_ref, q_ref, kn_ref, vn_ref, z_ref, nw_ref, bias_ref, biasn_ref, *rest,
                        n_pages, heads, dqk, nq, scale, post_scale):
    del pt_ref
    k_refs = rest[:n_pages]
    v_refs = rest[n_pages:2 * n_pages]
    o_ref = rest[2 * n_pages]
    s_scr = rest[2 * n_pages + 1]
    lam = scal_ref[0]
    dh = 2 * dqk
    rows_pp = k_refs[0].shape[0] * k_refs[0].shape[1]

    q = q_ref[0] * scale
    lane = lax.broadcasted_iota(jnp.int32, (nq, dh), 1)
    pieces = []
    for h in range(heads):
        qh = q[:, h * dh:(h + 1) * dh]
        pieces.append(jnp.where(lane < dqk, qh, 0.0))
        pieces.append(jnp.where(lane >= dqk, qh, 0.0))
    qall = jnp.concatenate(pieces, axis=0).astype(BF16)
    nrow = qall.shape[0]

    def scores(k2d):
        return lax.dot_general(qall, k2d, (((1,), (1,)), ((), ())), preferred_element_type=F32)

    bias = bias_ref[...]
    m = jnp.full((nrow, 1), NEG_BIG, F32)
    for j in range(n_pages):
        kj = k_refs[j][...].reshape(rows_pp, dh).astype(BF16)
        s = scores(kj) + bias
        s_scr[:, j * rows_pp:(j + 1) * rows_pp] = s
        m = jnp.maximum(m, jnp.max(s, axis=-1, keepdims=True))
    sn = scores(kn_ref[...].astype(BF16)) + biasn_ref[...]
    m = jnp.maximum(m, jnp.max(sn, axis=-1, keepdims=True))

    l = jnp.zeros((nrow, 1), F32)
    acc = jnp.zeros((nrow, dh), F32)
    for j in range(n_pages):
        p = jnp.exp(s_scr[:, j * rows_pp:(j + 1) * rows_pp] - m)
        l = l + jnp.sum(p, axis=-1, keepdims=True)
        vj = v_refs[j][...].reshape(rows_pp, dh).astype(BF16)
        acc = acc + jnp.dot(p.astype(BF16), vj, preferred_element_type=F32)
    pn = jnp.exp(sn - m)
    l = l + jnp.sum(pn, axis=-1, keepdims=True)
    acc = acc + jnp.dot(pn.astype(BF16), vn_ref[...].astype(BF16), preferred_element_type=F32)

    o = acc / l
    z = z_ref[0]
    nw = nw_ref[...]
    for h in range(heads):
        r0 = h * 2 * nq
        oh = o[r0:r0 + nq] - lam * o[r0 + nq:r0 + 2 * nq]
        ms = jnp.mean(oh * oh, axis=-1, keepdims=True)
        y = oh * lax.rsqrt(ms + EPS) * nw[:, h * dh:(h + 1) * dh] * post_scale
        o_ref[0, :, h * dh:(h + 1) * dh] = (y * _silu(z[:, h * dh:(h + 1) * dh])).astype(o_ref.dtype)


def _sample_masks(heads, nq, page_rows):
    r = np.arange(heads * 2 * nq)
    rh, rt = r // (2 * nq), r % nq
    c = np.arange(page_rows)
    past = np.where((c[None, :] % heads) == rh[:, None], 0.0, -np.inf).astype(np.float32)
    cn = np.arange(nq * heads)
    ok = ((cn[None, :] % heads) == rh[:, None]) & ((cn[None, :] // heads) <= rt[:, None])
    new = np.where(ok, 0.0, -np.inf).astype(np.float32)
    return jnp.asarray(past), jnp.asarray(new)


def sample_attention(u, k_rows, v_rows, kv_layer, cache_k, cache_v, layer, page_table, scal, norm_w,
                     *, dqk, q_col, z_col, post_scale):
    b, nq, _ = u.shape
    _, _, page, heads, dh = cache_k.shape
    n_pages = page_table.shape[1]
    width = heads * dh
    qb, zb = q_col // width, z_col // width
    bias, bias_new = _sample_masks(heads, nq, page * heads)
    nrow = heads * 2 * nq

    def page_spec(j):
        return pl.BlockSpec((None, None, page, heads, dh), lambda i, pt: (layer, pt[i, j], 0, 0, 0))

    new_spec = pl.BlockSpec((None, nq * heads, dh), lambda i, pt: (kv_layer, i, 0))
    kern = functools.partial(_sample_attn_kernel, n_pages=n_pages, heads=heads, dqk=dqk, nq=nq,
                             scale=dqk ** -0.5, post_scale=post_scale)
    grid_spec = pltpu.PrefetchScalarGridSpec(
        num_scalar_prefetch=1,
        grid=(b,),
        in_specs=[
            pl.BlockSpec(memory_space=pltpu.SMEM),
            pl.BlockSpec((1, nq, width), lambda i, pt: (i, 0, qb)),
            new_spec, new_spec,
            pl.BlockSpec((1, nq, width), lambda i, pt: (i, 0, zb)),
            pl.BlockSpec((1, width), lambda i, pt: (0, 0)),
            pl.BlockSpec(bias.shape, lambda i, pt: (0, 0)),
            pl.BlockSpec(bias_new.shape, lambda i, pt: (0, 0)),
        ] + [page_spec(j) for j in range(n_pages)] * 2,
        out_specs=pl.BlockSpec((1, nq, width), lambda i, pt: (i, 0, 0)),
        scratch_shapes=[pltpu.VMEM((nrow, n_pages * page * heads), F32)],
    )
    return pl.pallas_call(
        kern,
        grid_spec=grid_spec,
        out_shape=jax.ShapeDtypeStruct((b, nq, width), BF16),
        compiler_params=_params(("arbitrary",)),
        name="sample_attention",
    )(page_table, scal, u, k_rows, v_rows, u, norm_w.reshape(1, width), bias, bias_new,
      *([cache_k] * n_pages), *([cache_v] * n_pages))


def _mlstm_kernel(qk_ref, v_ref, og_ref, z_ref, if_ref, cw_ref, cb_ref, bif_ref, nw_ref,
                  c0_ref, n0_ref, m0_ref, buf0_ref, *rest, n_prev, heads, taps):
    om_ref, c_ref, n_ref, m_ref, tail_ref, ext_scr, m_scr = rest[n_prev:]
    ci = pl.program_id(1)
    nc = pl.num_programs(1)
    nb, L = qk_ref.shape[0], qk_ref.shape[1]
    mw = v_ref.shape[2]
    dh = mw // heads
    pad = SUBLANES

    @pl.when(ci == 0)
    def _():
        c_ref[0] = c0_ref[...]
        n_ref[0] = n0_ref[...]
        for bb in range(nb):
            for h in range(heads):
                m_scr[bb * heads + h] = jnp.broadcast_to(m0_ref[bb, h:h + 1, :], m_scr.shape[1:])
            ext_scr[bb, pad - (taps - 1):pad, :] = buf0_ref[bb]

    row = lax.broadcasted_iota(jnp.int32, (L, L), 0)
    col = lax.broadcasted_iota(jnp.int32, (L, L), 1)
    eye = row == col
    low = col <= row

    for bb in range(nb):
        ext_scr[bb, pad:pad + L, :] = qk_ref[bb]
        conv = cb_ref[...]
        for j in range(taps):
            r0 = pad - (taps - 1) + j
            conv = conv + ext_scr[bb, r0:r0 + L, :] * cw_ref[j:j + 1, :]

        ext_scr[bb, 0:pad, :] = ext_scr[bb, L:L + pad, :]
        qk = _silu(conv)
        q_all = qk[:, :mw]
        k_all = qk[:, mw:] * (dh ** -0.5)
        v_all = v_ref[bb]

        g = if_ref[bb] + bif_ref[...]
        lf_all = jnp.minimum(g, 0.0) - jnp.log(1.0 + jnp.exp(-jnp.abs(g)))

        for h in range(heads):
            li_col = g[:, h:h + 1]
            lf_col = lf_all[:, heads + h:heads + h + 1]
            lf_row = jnp.sum(jnp.where(eye, lf_col, 0.0), axis=0, keepdims=True)
            bt_col = jnp.sum(jnp.where(low, lf_row, 0.0), axis=1, keepdims=True)
            r_row = jnp.sum(jnp.where(eye, li_col - bt_col, 0.0), axis=0, keepdims=True)
            m_prev = m_scr[bb * heads + h, 0:1, 0:1]

            log_d = jnp.where(low, bt_col + r_row, -jnp.inf)
            log_inter = bt_col + m_prev
            m_t = jnp.maximum(log_inter, jnp.max(log_d, axis=1, keepdims=True))
            d_mat = jnp.exp(log_d - m_t)
            inter = jnp.exp(log_inter - m_t)

            sl = slice(h * dh, (h + 1) * dh)
            q_h, k_h = q_all[:, sl], k_all[:, sl]
            q_b, k_b, v_b = q_h.astype(BF16), k_h.astype(BF16), v_all[:, sl].astype(BF16)
            c_h = c_ref[0, bb, h]
            n_h = n_ref[0, bb, h:h + 1, :]

            s = lax.dot_general(q_b, k_b, (((1,), (1,)), ((), ())), preferred_element_type=F32) * d_mat
            num = (jnp.dot(s.astype(BF16), v_b, preferred_element_type=F32)
                   + jnp.dot(q_b, c_h.astype(BF16), preferred_element_type=F32) * inter)
            den = jnp.sum(s, axis=1, keepdims=True) + inter * jnp.sum(q_h * n_h, axis=1, keepdims=True)
            den = jnp.maximum(jnp.abs(den), jnp.exp(-m_t))
            hh = num / den

            b_last = jnp.sum(lf_col, axis=0, keepdims=True)
            log_w = b_last - bt_col + li_col
            m_new = jnp.maximum(b_last + m_prev, jnp.max(log_w, axis=0, keepdims=True))
            wk = jnp.exp(log_w - m_new)
            decay = jnp.exp(b_last + m_prev - m_new)
            kw = k_h * wk
            upd = lax.dot_general(kw.astype(BF16), v_b, (((0,), (0,)), ((), ())), preferred_element_type=F32)
            c_ref[0, bb, h] = decay * c_h + upd
            n_ref[0, bb, h:h + 1, :] = decay * n_h + jnp.sum(kw, axis=0, keepdims=True)
            m_scr[bb * heads + h] = jnp.broadcast_to(m_new, m_scr.shape[1:])

            mu = jnp.mean(hh, axis=1, keepdims=True)
            xc = hh - mu
            var = jnp.mean(xc * xc, axis=1, keepdims=True)
            y = xc * lax.rsqrt(var + EPS) * nw_ref[:, sl] * _sigmoid(og_ref[bb, :, sl])
            om_ref[bb, :, sl] = (y * _silu(z_ref[bb, :, sl])).astype(om_ref.dtype)

    @pl.when(ci == nc - 1)
    def _():
        for s in range(c_ref.shape[0]):
            if s > 0:
                c_ref[s] = c_ref[0]
                n_ref[s] = n_ref[0]
            for bb in range(nb):
                tail_ref[s, bb] = ext_scr[bb, pad - (taps - 1):pad, :]
                for h in range(heads):
                    m_ref[s, bb, h:h + 1, :] = m_scr[bb * heads + h, 0:1, :]


def mlstm_branch(u, u_if, conv_w, conv_b, b_if, norm_w, c0, n0, m0, buf0, *, state_layer, layer, depth, prev,
                 chunk, nb, qk_col, v_col, og_col, z_col):
    b, t, _ = u.shape
    _, _, heads, dh, _ = c0.shape
    mw = heads * dh
    taps = conv_w.shape[0]
    assert chunk >= taps - 1 and t % chunk == 0 and b % nb == 0
    bif_pad = jnp.zeros((1, LANES), F32).at[0, :2 * heads].set(b_if)
    kern = functools.partial(_mlstm_kernel, n_prev=0 if prev is None else 4, heads=heads, taps=taps)
    const2 = lambda bi, ci: (0, 0)
    st_in = lambda bi, ci: (state_layer, bi, 0, 0)
    nslab, slab = _slabs(prev, layer, depth)
    st_out = lambda bi, ci: (slab, bi, 0, 0)
    res = _stacked_call(
        kern,
        grid=(b // nb, t // chunk),
        in_specs=[
            pl.BlockSpec((nb, chunk, 2 * mw), lambda bi, ci: (bi, ci, qk_col // (2 * mw))),
            pl.BlockSpec((nb, chunk, mw), lambda bi, ci: (bi, ci, v_col // mw)),
            pl.BlockSpec((nb, chunk, mw), lambda bi, ci: (bi, ci, og_col // mw)),
            pl.BlockSpec((nb, chunk, mw), lambda bi, ci: (bi, ci, z_col // mw)),
            pl.BlockSpec((nb, chunk, LANES), lambda bi, ci: (bi, ci, 0)),
            pl.BlockSpec((taps, 2 * mw), const2),
            pl.BlockSpec((1, 2 * mw), const2),
            pl.BlockSpec((1, LANES), const2),
            pl.BlockSpec((1, mw), const2),
            pl.BlockSpec((None, nb, heads, dh, dh), lambda bi, ci: (state_layer, bi, 0, 0, 0)),
            pl.BlockSpec((None, nb, heads, dh), st_in),
            pl.BlockSpec((None, nb, heads, LANES), st_in),
            pl.BlockSpec((None, nb, taps - 1, 2 * mw), st_in),
        ],
        inputs=[u, u, u, u, u_if, conv_w, conv_b.reshape(1, 2 * mw), bif_pad, norm_w.reshape(1, mw),
                c0, n0, m0, buf0],
        out_specs=[
            pl.BlockSpec((nb, chunk, mw), lambda bi, ci: (bi, ci, 0)),
            pl.BlockSpec((nslab, nb, heads, dh, dh), lambda bi, ci: (slab, bi, 0, 0, 0)),
            pl.BlockSpec((nslab, nb, heads, dh), st_out),
            pl.BlockSpec((nslab, nb, heads, LANES), st_out),
            pl.BlockSpec((nslab, nb, taps - 1, 2 * mw), st_out),
        ],
        out_shape=[
            jax.ShapeDtypeStruct((b, t, mw), BF16),
            jax.ShapeDtypeStruct((depth, b, heads, dh, dh), F32),
            jax.ShapeDtypeStruct((depth, b, heads, dh), F32),
            jax.ShapeDtypeStruct((depth, b, heads, LANES), F32),
            jax.ShapeDtypeStruct((depth, b, taps - 1, 2 * mw), F32),
        ],
        n_stacked=4, prev=prev,
        scratch_shapes=[pltpu.VMEM((nb, chunk + SUBLANES, 2 * mw), F32),
                        pltpu.VMEM((nb * heads, SUBLANES, LANES), F32)],
        compiler_params=_params(("parallel", "arbitrary")),
        name="mlstm_branch",
    )
    return res[0], tuple(res[1:])


def _layer_group(x, lw, lam, attend, states, *, layer, depth, prev, chunk, nb, tm):
    b, t, d = x.shape
    heads, aw, mw, col = lw["heads"], lw["aw"], lw["mw"], lw["col"]
    x2 = x.reshape(b * t, d)
    u2, uif2, k_st, v_st = in_projection(
        x2, lw["norm_pre"], lw["w_t"], lw["w_if"], layer=layer, depth=depth, heads=heads,
        k_col=col["ak"], v_col=col["av"], prev_kv=None if prev is None else prev[0], tm=tm, tn=aw)
    u = u2.reshape(b, t, -1)
    scal = jnp.reshape(lam, (1,)).astype(F32)
    o_a = attend(u, k_st, v_st, scal)
    c0, n0, m0, buf0, state_layer = states
    o_m, st = mlstm_branch(u, uif2.reshape(b, t, LANES), lw["conv_w"], lw["conv_b"], lw["b_if"],
                           lw["mlstm_norm_w"], c0, n0, m0, buf0, state_layer=state_layer, layer=layer,
                           depth=depth, prev=None if prev is None else prev[1], chunk=chunk, nb=nb,
                           qk_col=col["mq"], v_col=col["mv"], og_col=col["mo"], z_col=col["mz"])
    y = gated_merge(o_a.reshape(b * t, aw), o_m.reshape(b * t, mw), lw["w_pa"], lw["w_pm"], u2,
                    col["ga"], col["gm"], tm=tm, tn=1024)
    x_new = out_projection(y, lw["w_out"], lw["norm_post"], x2, tm=min(tm, 512)).reshape(b, t, d)
    return x_new, ((k_st, v_st), st)


def kernel(x_prompt, x_sample, cache_k, cache_v, state_C, state_n, state_m, state_conv, page_table,
           norm_pre, norm_post, w_in, b_if, conv_w, conv_b, lambda_qk, attn_norm_w, mlstm_norm_w,
           w_pa, w_pm, w_out):
    depth = w_in.shape[0]
    d_model = x_prompt.shape[-1]
    heads = cache_k.shape[3]
    dqk = lambda_qk.shape[-1]
    dh = 2 * dqk
    aw = heads * dh
    m_heads, m_dh = state_C.shape[2], state_C.shape[3]
    mw = m_heads * m_dh
    bp, sp, _ = x_prompt.shape
    bs, ts, _ = x_sample.shape
    taps = conv_w.shape[1]
    n_if = 2 * m_heads
    if_col = 4 * aw + 4 * mw
    col = {"aq": 0, "ak": aw, "av": 2 * aw, "az": 3 * aw, "mq": 4 * aw, "mv": 4 * aw + 2 * mw,
           "mo": 4 * aw + 3 * mw, "mz": if_col, "ga": if_col + mw, "gm": if_col + mw + d_model}

    zero_states = (jnp.zeros((1, bp, m_heads, m_dh, m_dh), F32), jnp.zeros((1, bp, m_heads, m_dh), F32),
                   jnp.zeros((1, bp, m_heads, LANES), F32), jnp.zeros((1, bp, taps - 1, 2 * mw), F32))
    m_rep = jnp.broadcast_to(state_m[..., None], state_m.shape + (LANES,))

    xp, xs = x_prompt, x_sample
    prev_p = prev_s = None
    for l in range(depth):
        lam_init = 0.8 - 0.6 * math.exp(-0.3 * l)
        lq = lambda_qk[l].astype(F32)
        lam = jnp.exp(jnp.sum(lq[0] * lq[1])) - jnp.exp(jnp.sum(lq[2] * lq[3])) + lam_init
        w_l = w_in[l]
        lw = dict(heads=heads, aw=aw, mw=mw, col=col, norm_pre=norm_pre[l], norm_post=norm_post[l],
                  w_t=jnp.concatenate([jnp.swapaxes(w_l[:, :if_col], 0, 1),
                                       jnp.swapaxes(w_l[:, if_col + n_if:], 0, 1)], axis=0).astype(BF16),
                  w_if=jnp.pad(jnp.swapaxes(w_l[:, if_col:if_col + n_if], 0, 1),
                               ((0, LANES - n_if), (0, 0))).astype(BF16),
                  conv_w=conv_w[l], conv_b=conv_b[l], b_if=b_if[l], mlstm_norm_w=mlstm_norm_w[l],
                  w_pa=w_pa[l].astype(BF16), w_pm=w_pm[l].astype(BF16), w_out=w_out[l].astype(BF16))
        post = 1.0 - lam_init

        def attend_prompt(u, k_st, v_st, scal):
            return prompt_attention(u, scal, attn_norm_w[l], heads=heads, dqk=dqk, q_col=col["aq"],
                                    k_col=col["ak"], v_col=col["av"], z_col=col["az"], post_scale=post, tq=512)

        def attend_sample(u, k_st, v_st, scal):
            return sample_attention(u, k_st, v_st, l, cache_k, cache_v, l, page_table, scal, attn_norm_w[l],
                                    dqk=dqk, q_col=col["aq"], z_col=col["az"], post_scale=post)

        xp, prev_p = _layer_group(xp, lw, lam, attend_prompt, zero_states + (0,), layer=l, depth=depth,
                                  prev=prev_p, chunk=CHUNK if sp % CHUNK == 0 else sp, nb=1, tm=1024)
        xs, prev_s = _layer_group(xs, lw, lam, attend_sample, (state_C, state_n, m_rep, state_conv, l), layer=l,
                                  depth=depth, prev=prev_s, chunk=CHUNK if ts % CHUNK == 0 else ts, nb=4, tm=bs * ts)

    def unpack(prev, b, t):
        (k_st, v_st), (c, n, m, tail) = prev
        return (k_st.reshape(depth, b, t, heads, dh), v_st.reshape(depth, b, t, heads, dh), c, n, m[..., 0], tail)

    return (xp, xs) + unpack(prev_p, bp, sp) + unpack(prev_s, bs, ts)
```

```python
import functools
import math

import jax
import jax.numpy as jnp
import numpy as np
from jax import lax
from jax.experimental import pallas as pl
from jax.experimental.pallas import tpu as pltpu

F32 = jnp.float32
BF16 = jnp.bfloat16

EPS = 1e-6
CHUNK = 64
LANES = 128
SUBLANES = 8
BF16_ROWS = 16
NEG_BIG = -1e30
VMEM_LIMIT = 56 * 1024 * 1024
LOG2E = math.log2(math.e)


def _sigmoid(x):
    return 1.0 / (1.0 + jnp.exp(-x))


def _silu(x):
    return x * _sigmoid(x)


def _params(sem):
    return pltpu.CompilerParams(dimension_semantics=sem, vmem_limit_bytes=VMEM_LIMIT)


def _slabs(prev, layer, depth):
    return (depth, 0) if prev is None else (1, layer)


def _stacked_call(kernel_fn, *, grid, in_specs, inputs, out_specs, out_shape, n_stacked, prev, **kw):
    aliases = {}
    if prev is not None:
        first = len(out_shape) - n_stacked
        for k, p in enumerate(prev):
            aliases[len(inputs) + k] = first + k
        in_specs = list(in_specs) + [pl.BlockSpec(memory_space=pl.ANY)] * len(prev)
        inputs = list(inputs) + list(prev)
    return pl.pallas_call(kernel_fn, grid=grid, in_specs=in_specs, out_specs=out_specs, out_shape=out_shape,
                          input_output_aliases=aliases, **kw)(*inputs)


def _dot_nt(a, b):
    return lax.dot_general(a, b, (((1,), (1,)), ((), ())), preferred_element_type=F32)


def _inproj_kernel(x_ref, nw_ref, w_ref, wif_ref, *rest, n_prev, k_blk, v_blk, heads, nslab, slab):
    u_ref, uif_ref, k_hbm, v_hbm, h_scr, k_stage, v_stage, sems = rest[n_prev:]
    i, j = pl.program_id(0), pl.program_id(1)
    tm = x_ref.shape[0]
    dh = k_stage.shape[-1]
    rows = tm * heads
    row0 = pl.multiple_of(i * rows, rows)
    targets = ((k_blk, k_stage, k_hbm), (v_blk, v_stage, v_hbm))

    def copies(which):
        _, stage, hbm = targets[which]
        return [pltpu.make_async_copy(stage, hbm.at[slab + s, pl.ds(row0, rows), :], sems.at[which, s])
                for s in range(nslab)]

    @pl.when(j == 0)
    def _():
        x = x_ref[...]
        ms = jnp.mean(x * x, axis=-1, keepdims=True)
        h = (x * lax.rsqrt(ms + EPS) * nw_ref[...]).astype(BF16)
        h_scr[...] = h
        uif_ref[...] = _dot_nt(h, wif_ref[...].astype(BF16))

    acc = _dot_nt(h_scr[...], w_ref[...])
    u_ref[...] = acc

    for which, (blk, stage, _) in enumerate(targets):
        @pl.when(j == blk)
        def _():
            for h in range(heads):
                stage[pl.ds(h, tm, stride=heads), :] = acc[:, h * dh:(h + 1) * dh]
            for c in copies(which):
                c.start()

    @pl.when(j == pl.num_programs(1) - 1)
    def _():
        for which in range(len(targets)):
            for c in copies(which):
                c.wait()


def in_projection(x, norm_w, w_t, w_if, *, layer, depth, heads, k_col, v_col, prev_kv, tm, tn):
    t, d = x.shape
    n = w_t.shape[1]
    dh = tn // heads
    k_blk, v_blk = k_col // tn, v_col // tn
    assert k_col % tn == 0 and v_col % tn == 0 and tn == heads * dh and max(k_blk, v_blk) < n // tn - 1
    nslab, slab = _slabs(prev_kv, layer, depth)
    kern = functools.partial(_inproj_kernel, n_prev=0 if prev_kv is None else 2, k_blk=k_blk, v_blk=v_blk,
                             heads=heads, nslab=nslab, slab=slab)
    kv_spec = pl.BlockSpec(memory_space=pl.ANY)
    kv_shape = jax.ShapeDtypeStruct((depth, t * heads, dh), F32)
    stage = pltpu.VMEM((tm * heads, dh), F32)
    return _stacked_call(
        kern,
        grid=(t // tm, n // tn),
        in_specs=[
            pl.BlockSpec((tm, d), lambda i, j: (i, 0)),
            pl.BlockSpec((1, d), lambda i, j: (0, 0)),
            pl.BlockSpec((None, tn, d), lambda i, j: (layer, j, 0)),
            pl.BlockSpec((LANES, d), lambda i, j: (0, 0)),
        ],
        inputs=[x, norm_w.reshape(1, d), w_t, w_if],
        out_specs=[
            pl.BlockSpec((tm, tn), lambda i, j: (i, j)),
            pl.BlockSpec((tm, LANES), lambda i, j: (i, 0)),
            kv_spec, kv_spec,
        ],
        out_shape=[jax.ShapeDtypeStruct((t, n), F32), jax.ShapeDtypeStruct((t, LANES), F32), kv_shape, kv_shape],
        n_stacked=2, prev=prev_kv,
        scratch_shapes=[pltpu.VMEM((tm, d), BF16), stage, stage, pltpu.SemaphoreType.DMA((2, nslab))],
        compiler_params=_params(("arbitrary", "arbitrary")),
        name="in_projection",
    )


def _merge_kernel(oa_ref, om_ref, wpa_ref, wpm_ref, ga_ref, gm_ref, y_ref):
    pa = jnp.dot(oa_ref[...], wpa_ref[...], preferred_element_type=F32)
    pm = jnp.dot(om_ref[...], wpm_ref[...], preferred_element_type=F32)
    y = _sigmoid(ga_ref[...]) * pa + _sigmoid(gm_ref[...]) * pm
    y_ref[...] = y.astype(y_ref.dtype)


def gated_merge(o_a, o_m, w_pa, w_pm, u, ga_col, gm_col, *, tm, tn):
    t, wa = o_a.shape
    wm = o_m.shape[1]
    d = w_pa.shape[1]
    ga_blk, gm_blk = ga_col // tn, gm_col // tn
    return pl.pallas_call(
        _merge_kernel,
        grid=(t // tm, d // tn),
        in_specs=[
            pl.BlockSpec((tm, wa), lambda i, j: (i, 0)),
            pl.BlockSpec((tm, wm), lambda i, j: (i, 0)),
            pl.BlockSpec((wa, tn), lambda i, j: (0, j)),
            pl.BlockSpec((wm, tn), lambda i, j: (0, j)),
            pl.BlockSpec((tm, tn), lambda i, j: (i, ga_blk + j)),
            pl.BlockSpec((tm, tn), lambda i, j: (i, gm_blk + j)),
        ],
        out_specs=pl.BlockSpec((tm, tn), lambda i, j: (i, j)),
        out_shape=jax.ShapeDtypeStruct((t, d), BF16),
        compiler_params=_params(("parallel", "arbitrary")),
        name="gated_merge",
    )(o_a, o_m, w_pa, w_pm, u, u)


def _outproj_kernel(y_ref, w_ref, nw_ref, x_ref, o_ref):
    out = jnp.dot(y_ref[...], w_ref[...], preferred_element_type=F32)
    ms = jnp.mean(out * out, axis=-1, keepdims=True)
    o_ref[...] = x_ref[...] + out * lax.rsqrt(ms + EPS) * nw_ref[...]


def out_projection(y, w_out, norm_w, x, *, tm):
    t, d = x.shape
    return pl.pallas_call(
        _outproj_kernel,
        grid=(t // tm,),
        in_specs=[
            pl.BlockSpec((tm, d), lambda i: (i, 0)),
            pl.BlockSpec((d, d), lambda i: (0, 0)),
            pl.BlockSpec((1, d), lambda i: (0, 0)),
            pl.BlockSpec((tm, d), lambda i: (i, 0)),
        ],
        out_specs=pl.BlockSpec((tm, d), lambda i: (i, 0)),
        out_shape=jax.ShapeDtypeStruct((t, d), F32),
        compiler_params=_params(("parallel",)),
        name="out_projection",
    )(y, w_out, norm_w.reshape(1, d), x)


def _prompt_attn_kernel(*refs, **kw):
    _prompt_attn_body(pl.program_id(2), *refs, **kw)


def _prompt_attn_body(qi, scal_ref, q_ref, k_ref, v_ref, z_ref, nw_ref, o_ref, kb_scr, vt_scr,
                      *, tq, dqk, scale_log2, post_scale):
    lam = scal_ref[0]
    dv = v_ref.shape[-1]
    n_blk = vt_scr.shape[0]

    @pl.when(qi == 0)
    def _():
        kb_scr[...] = k_ref[0].astype(BF16)
        for c in range(n_blk):
            vt_scr[c, 0:dv, :] = jnp.transpose(v_ref[0, c * tq:(c + 1) * tq, :]).astype(BF16)
            vt_scr[c, dv:, :] = jnp.ones((BF16_ROWS, tq), BF16)

    q = q_ref[0] * scale_log2
    lane = lax.broadcasted_iota(jnp.int32, q.shape, 1)
    q1 = jnp.where(lane < dqk, q, 0.0).astype(BF16)
    q2 = jnp.where(lane >= dqk, q, 0.0).astype(BF16)

    def scores_t(k, qm):
        return lax.dot_general(k, qm, (((1,), (1,)), ((), ())), preferred_element_type=F32)

    def online(state, s_t, vt):
        m, acc = state
        m_new = jnp.maximum(m, jnp.max(s_t, axis=0, keepdims=True))
        alpha = jnp.exp2(m - m_new)
        p = jnp.exp2(s_t - m_new)
        acc = alpha * acc + jnp.dot(vt, p.astype(BF16), preferred_element_type=F32)
        return m_new, acc

    def scores_of(j):
        k = kb_scr[pl.ds(pl.multiple_of(j * tq, tq), tq), :]
        return scores_t(k, q1), scores_t(k, q2)

    def causal(s):
        krow = lax.broadcasted_iota(jnp.int32, s.shape, 0)
        qcol = lax.broadcasted_iota(jnp.int32, s.shape, 1)
        return jnp.where(krow <= qcol, s, -jnp.inf)

    def pair(j, carry, second_is_diagonal):
        st1, st2 = carry
        sa1, sa2 = scores_of(j)
        sb1, sb2 = scores_of(j + 1)
        if second_is_diagonal:
            sb1, sb2 = causal(sb1), causal(sb2)
        vta, vtb = vt_scr[j], vt_scr[j + 1]
        return online(online(st1, sa1, vta), sb1, vtb), online(online(st2, sa2, vta), sb2, vtb)

    def diagonal_only(carry):
        st1, st2 = carry
        s1, s2 = scores_of(qi)
        vt = vt_scr[qi]
        return online(st1, causal(s1), vt), online(st2, causal(s2), vt)

    def init():
        return jnp.full((1, tq), NEG_BIG, F32), jnp.zeros((dv + BF16_ROWS, tq), F32)

    carry = lax.fori_loop(0, qi // 2, lambda p, c: pair(2 * p, c, False), (init(), init()))
    (_, a1), (_, a2) = lax.cond(qi % 2 == 1, lambda c: pair(qi - 1, c, True), diagonal_only, carry)

    out_t = a1[0:dv] / a1[dv:dv + 1] - lam * (a2[0:dv] / a2[dv:dv + 1])
    out = jnp.transpose(out_t)
    ms = jnp.mean(out * out, axis=-1, keepdims=True)
    y = out * lax.rsqrt(ms + EPS) * nw_ref[...] * post_scale
    o_ref[0] = (y * _silu(z_ref[0])).astype(o_ref.dtype)


def prompt_attention(u, scal, norm_w, *, heads, dqk, q_col, k_col, v_col, z_col, post_scale, tq):
    b, s, _ = u.shape
    dh = 2 * dqk
    qb, kb, vb, zb = q_col // dh, k_col // dh, v_col // dh, z_col // dh
    kern = functools.partial(_prompt_attn_kernel, tq=tq, dqk=dqk, scale_log2=dqk ** -0.5 * LOG2E,
                             post_scale=post_scale)
    return pl.pallas_call(
        kern,
        grid=(b, heads, s // tq),
        in_specs=[
            pl.BlockSpec(memory_space=pltpu.SMEM),
            pl.BlockSpec((1, tq, dh), lambda bi, h, i: (bi, i, qb + h)),
            pl.BlockSpec((1, s, dh), lambda bi, h, i: (bi, 0, kb + h)),
            pl.BlockSpec((1, s, dh), lambda bi, h, i: (bi, 0, vb + h)),
            pl.BlockSpec((1, tq, dh), lambda bi, h, i: (bi, i, zb + h)),
            pl.BlockSpec((1, dh), lambda bi, h, i: (0, h)),
        ],
        out_specs=pl.BlockSpec((1, tq, dh), lambda bi, h, i: (bi, i, h)),
        out_shape=jax.ShapeDtypeStruct((b, s, heads * dh), BF16),
        scratch_shapes=[pltpu.VMEM((s, dh), BF16), pltpu.VMEM((s // tq, dh + BF16_ROWS, tq), BF16)],
        compiler_params=_params(("parallel", "parallel", "arbitrary")),
        name="prompt_attention",
    )(scal, u, u, u, u, norm_w.reshape(1, heads * dh))


def _sample_attn_kernel(pt_ref, scal_ref, q_ref, kn_ref, vn_ref, z_ref, nw_ref, bias_ref, biasn_ref, *rest,
                        n_pages, **kw):
    del pt_ref
    _sample_attn_body(scal_ref, q_ref, kn_ref, vn_ref, z_ref, nw_ref, bias_ref, biasn_ref,
                      rest[:n_pages], rest[n_pages:2 * n_pages], rest[2 * n_pages], rest[2 * n_pages + 1], **kw)


def _sample_attn_body(scal_ref, q_ref, kn_ref, vn_ref, z_ref, nw_ref, bias_ref, biasn_ref, k_refs, v_refs,
                      o_ref, s_scr, *, heads, dqk, nq, scale, post_scale):
    n_pages = len(k_refs)
    lam = scal_ref[0]
    dh = 2 * dqk
    rows_pp = k_refs[0].shape[0] * k_refs[0].shape[1]

    q = q_ref[0] * scale
    lane = lax.broadcasted_iota(jnp.int32, (nq, dh), 1)
    pieces = []
    for h in range(heads):
        qh = q[:, h * dh:(h + 1) * dh]
        pieces.append(jnp.where(lane < dqk, qh, 0.0))
        pieces.append(jnp.where(lane >= dqk, qh, 0.0))
    qall = jnp.concatenate(pieces, axis=0).astype(BF16)
    nrow = qall.shape[0]

    def scores(k2d):
        return lax.dot_general(qall, k2d, (((1,), (1,)), ((), ())), preferred_element_type=F32)

    bias = bias_ref[...]
    m = jnp.full((nrow, 1), NEG_BIG, F32)
    for j in range(n_pages):
        kj = k_refs[j][...].reshape(rows_pp, dh).astype(BF16)
        s = scores(kj) + bias
        s_scr[:, j * rows_pp:(j + 1) * rows_pp] = s
        m = jnp.maximum(m, jnp.max(s, axis=-1, keepdims=True))
    sn = scores(kn_ref[...].astype(BF16)) + biasn_ref[...]
    m = jnp.maximum(m, jnp.max(sn, axis=-1, keepdims=True))

    l = jnp.zeros((nrow, 1), F32)
    acc = jnp.zeros((nrow, dh), F32)
    for j in range(n_pages):
        p = jnp.exp(s_scr[:, j * rows_pp:(j + 1) * rows_pp] - m)
        l = l + jnp.sum(p, axis=-1, keepdims=True)
        vj = v_refs[j][...].reshape(rows_pp, dh).astype(BF16)
        acc = acc + jnp.dot(p.astype(BF16), vj, preferred_element_type=F32)
    pn = jnp.exp(sn - m)
    l = l + jnp.sum(pn, axis=-1, keepdims=True)
    acc = acc + jnp.dot(pn.astype(BF16), vn_ref[...].astype(BF16), preferred_element_type=F32)

    o = acc / l
    z = z_ref[0]
    nw = nw_ref[...]
    for h in range(heads):
        r0 = h * 2 * nq
        oh = o[r0:r0 + nq] - lam * o[r0 + nq:r0 + 2 * nq]
        ms = jnp.mean(oh * oh, axis=-1, keepdims=True)
        y = oh * lax.rsqrt(ms + EPS) * nw[:, h * dh:(h + 1) * dh] * post_scale
        o_ref[0, :, h * dh:(h + 1) * dh] = (y * _silu(z[:, h * dh:(h + 1) * dh])).astype(o_ref.dtype)


def _sample_masks(heads, nq, page_rows):
    r = np.arange(heads * 2 * nq)
    rh, rt = r // (2 * nq), r % nq
    c = np.arange(page_rows)
    past = np.where((c[None, :] % heads) == rh[:, None], 0.0, -np.inf).astype(np.float32)
    cn = np.arange(nq * heads)
    ok = ((cn[None, :] % heads) == rh[:, None]) & ((cn[None, :] // heads) <= rt[:, None])
    new = np.where(ok, 0.0, -np.inf).astype(np.float32)
    return jnp.asarray(past), jnp.asarray(new)


def sample_attention(u, k_rows, v_rows, kv_layer, cache_k, cache_v, layer, page_table, scal, norm_w,
                     *, dqk, q_col, z_col, post_scale):
    b, nq, _ = u.shape
    _, _, page, heads, dh = cache_k.shape
    n_pages = page_table.shape[1]
    width = heads * dh
    qb, zb = q_col // width, z_col // width
    bias, bias_new = _sample_masks(heads, nq, page * heads)
    nrow = heads * 2 * nq

    def page_spec(j):
        return pl.BlockSpec((None, None, page, heads, dh), lambda i, pt: (layer, pt[i, j], 0, 0, 0))

    new_spec = pl.BlockSpec((None, nq * heads, dh), lambda i, pt: (kv_layer, i, 0))
    kern = functools.partial(_sample_attn_kernel, n_pages=n_pages, heads=heads, dqk=dqk, nq=nq,
                             scale=dqk ** -0.5, post_scale=post_scale)
    grid_spec = pltpu.PrefetchScalarGridSpec(
        num_scalar_prefetch=1,
        grid=(b,),
        in_specs=[
            pl.BlockSpec(memory_space=pltpu.SMEM),
            pl.BlockSpec((1, nq, width), lambda i, pt: (i, 0, qb)),
            new_spec, new_spec,
            pl.BlockSpec((1, nq, width), lambda i, pt: (i, 0, zb)),
            pl.BlockSpec((1, width), lambda i, pt: (0, 0)),
            pl.BlockSpec(bias.shape, lambda i, pt: (0, 0)),
            pl.BlockSpec(bias_new.shape, lambda i, pt: (0, 0)),
        ] + [page_spec(j) for j in range(n_pages)] * 2,
        out_specs=pl.BlockSpec((1, nq, width), lambda i, pt: (i, 0, 0)),
        scratch_shapes=[pltpu.VMEM((nrow, n_pages * page * heads), F32)],
    )
    return pl.pallas_call(
        kern,
        grid_spec=grid_spec,
        out_shape=jax.ShapeDtypeStruct((b, nq, width), BF16),
        compiler_params=_params(("arbitrary",)),
        name="sample_attention",
    )(page_table, scal, u, k_rows, v_rows, u, norm_w.reshape(1, width), bias, bias_new,
      *([cache_k] * n_pages), *([cache_v] * n_pages))


def _fused_attn_kernel(pt_ref, scal_ref, qp_ref, kp_ref, vp_ref, zp_ref, nwp_ref,
                       qs_ref, kn_ref, vn_ref, zs_ref, nws_ref, bias_ref, biasn_ref, *rest,
                       n_pages, n_qblk, heads, dqk, nq, tq, post_scale):
    del pt_ref
    k_refs, v_refs = rest[:n_pages], rest[n_pages:2 * n_pages]
    op_ref, os_ref, kb_scr, vt_scr, s_scr = rest[2 * n_pages:]
    _sample_attn_body(scal_ref, qs_ref, kn_ref, vn_ref, zs_ref, nws_ref, bias_ref, biasn_ref, k_refs, v_refs,
                      os_ref, s_scr, heads=heads, dqk=dqk, nq=nq, scale=dqk ** -0.5, post_scale=post_scale)
    _prompt_attn_body(pl.program_id(0) % n_qblk, scal_ref, qp_ref, kp_ref, vp_ref, zp_ref, nwp_ref, op_ref,
                      kb_scr, vt_scr, tq=tq, dqk=dqk, scale_log2=dqk ** -0.5 * LOG2E, post_scale=post_scale)


def fused_attention(u_p, u_s, k_rows, v_rows, kv_layer, cache_k, cache_v, layer, page_table, scal, norm_w,
                    *, dqk, q_col, k_col, v_col, z_col, post_scale, tq):
    bp, sp, _ = u_p.shape
    bs, nq, _ = u_s.shape
    _, _, page, heads, dh = cache_k.shape
    n_pages = page_table.shape[1]
    width = heads * dh
    n_qblk = sp // tq
    assert bp * heads * n_qblk == bs
    qb, kb, vb, zb = q_col // dh, k_col // dh, v_col // dh, z_col // dh
    qbs, zbs = q_col // width, z_col // width
    bias, bias_new = _sample_masks(heads, nq, page * heads)

    def pb(s):
        return s // (heads * n_qblk)

    def ph(s):
        return (s // n_qblk) % heads

    def pq(s):
        return s % n_qblk

    def page_spec(j):
        return pl.BlockSpec((None, None, page, heads, dh), lambda s, pt: (layer, pt[s, j], 0, 0, 0))

    new_spec = pl.BlockSpec((None, nq * heads, dh), lambda s, pt: (kv_layer, s, 0))
    kern = functools.partial(_fused_attn_kernel, n_pages=n_pages, n_qblk=n_qblk, heads=heads, dqk=dqk, nq=nq,
                             tq=tq, post_scale=post_scale)
    grid_spec = pltpu.PrefetchScalarGridSpec(
        num_scalar_prefetch=1,
        grid=(bs,),
        in_specs=[
            pl.BlockSpec(memory_space=pltpu.SMEM),
            pl.BlockSpec((1, tq, dh), lambda s, pt: (pb(s), pq(s), qb + ph(s))),
            pl.BlockSpec((1, sp, dh), lambda s, pt: (pb(s), 0, kb + ph(s))),
            pl.BlockSpec((1, sp, dh), lambda s, pt: (pb(s), 0, vb + ph(s))),
            pl.BlockSpec((1, tq, dh), lambda s, pt: (pb(s), pq(s), zb + ph(s))),
            pl.BlockSpec((1, dh), lambda s, pt: (0, ph(s))),
            pl.BlockSpec((1, nq, width), lambda s, pt: (s, 0, qbs)),
            new_spec, new_spec,
            pl.BlockSpec((1, nq, width), lambda s, pt: (s, 0, zbs)),
            pl.BlockSpec((1, width), lambda s, pt: (0, 0)),
            pl.BlockSpec(bias.shape, lambda s, pt: (0, 0)),
            pl.BlockSpec(bias_new.shape, lambda s, pt: (0, 0)),
        ] + [page_spec(j) for j in range(n_pages)] * 2,
        out_specs=[
            pl.BlockSpec((1, tq, dh), lambda s, pt: (pb(s), pq(s), ph(s))),
            pl.BlockSpec((1, nq, width), lambda s, pt: (s, 0, 0)),
        ],
        scratch_shapes=[pltpu.VMEM((sp, dh), BF16), pltpu.VMEM((n_qblk, dh + BF16_ROWS, tq), BF16),
                        pltpu.VMEM((heads * 2 * nq, n_pages * page * heads), F32)],
    )
    nw2 = norm_w.reshape(1, width)
    return pl.pallas_call(
        kern,
        grid_spec=grid_spec,
        out_shape=[jax.ShapeDtypeStruct((bp, sp, width), BF16), jax.ShapeDtypeStruct((bs, nq, width), BF16)],
        compiler_params=_params(("arbitrary",)),
        name="fused_attention",
    )(page_table, scal, u_p, u_p, u_p, u_p, nw2, u_s, k_rows, v_rows, u_s, nw2, bias, bias_new,
      *([cache_k] * n_pages), *([cache_v] * n_pages))


def _mlstm_kernel(qk_ref, v_ref, og_ref, z_ref, if_ref, cw_ref, cb_ref, bif_ref, nw_ref,
                  c0_ref, n0_ref, m0_ref, buf0_ref, *rest, n_prev, heads, taps):
    om_ref, c_ref, n_ref, m_ref, tail_ref, ext_scr, m_scr = rest[n_prev:]
    ci = pl.program_id(1)
    nc = pl.num_programs(1)
    nb, L = qk_ref.shape[0], qk_ref.shape[1]
    mw = v_ref.shape[2]
    dh = mw // heads
    pad = SUBLANES

    @pl.when(ci == 0)
    def _():
        c_ref[0] = c0_ref[...]
        n_ref[0] = n0_ref[...]
        for bb in range(nb):
            for h in range(heads):
                m_scr[bb * heads + h] = jnp.broadcast_to(m0_ref[bb, h:h + 1, :], m_scr.shape[1:])
            ext_scr[bb, pad - (taps - 1):pad, :] = buf0_ref[bb]

    row = lax.broadcasted_iota(jnp.int32, (L, L), 0)
    col = lax.broadcasted_iota(jnp.int32, (L, L), 1)
    eye = row == col
    low = col <= row

    for bb in range(nb):
        ext_scr[bb, pad:pad + L, :] = qk_ref[bb]
        conv = cb_ref[...]
        for j in range(taps):
            r0 = pad - (taps - 1) + j
            conv = conv + ext_scr[bb, r0:r0 + L, :] * cw_ref[j:j + 1, :]

        ext_scr[bb, 0:pad, :] = ext_scr[bb, L:L + pad, :]
        qk = _silu(conv)
        q_all = qk[:, :mw]
        k_all = qk[:, mw:] * (dh ** -0.5)
        v_all = v_ref[bb]

        g = if_ref[bb] + bif_ref[...]
        lf_all = jnp.minimum(g, 0.0) - jnp.log(1.0 + jnp.exp(-jnp.abs(g)))

        for h in range(heads):
            li_col = g[:, h:h + 1]
            lf_col = lf_all[:, heads + h:heads + h + 1]
            lf_row = jnp.sum(jnp.where(eye, lf_col, 0.0), axis=0, keepdims=True)
            bt_col = jnp.sum(jnp.where(low, lf_row, 0.0), axis=1, keepdims=True)
            r_row = jnp.sum(jnp.where(eye, li_col - bt_col, 0.0), axis=0, keepdims=True)
            m_prev = m_scr[bb * heads + h, 0:1, 0:1]

            log_d = jnp.where(low, bt_col + r_row, -jnp.inf)
            log_inter = bt_col + m_prev
            m_t = jnp.maximum(log_inter, jnp.max(log_d, axis=1, keepdims=True))
            d_mat = jnp.exp(log_d - m_t)
            inter = jnp.exp(log_inter - m_t)

            sl = slice(h * dh, (h + 1) * dh)
            q_h, k_h = q_all[:, sl], k_all[:, sl]
            q_b, k_b, v_b = q_h.astype(BF16), k_h.astype(BF16), v_all[:, sl].astype(BF16)
            c_h = c_ref[0, bb, h]
            n_h = n_ref[0, bb, h:h + 1, :]

            s = lax.dot_general(q_b, k_b, (((1,), (1,)), ((), ())), preferred_element_type=F32) * d_mat
            num = (jnp.dot(s.astype(BF16), v_b, preferred_element_type=F32)
                   + jnp.dot(q_b, c_h.astype(BF16), preferred_element_type=F32) * inter)
            den = jnp.sum(s, axis=1, keepdims=True) + inter * jnp.sum(q_h * n_h, axis=1, keepdims=True)
            den = jnp.maximum(jnp.abs(den), jnp.exp(-m_t))
            hh = num / den

            b_last = jnp.sum(lf_col, axis=0, keepdims=True)
            log_w = b_last - bt_col + li_col
            m_new = jnp.maximum(b_last + m_prev, jnp.max(log_w, axis=0, keepdims=True))
            wk = jnp.exp(log_w - m_new)
            decay = jnp.exp(b_last + m_prev - m_new)
            kw = k_h * wk
            upd = lax.dot_general(kw.astype(BF16), v_b, (((0,), (0,)), ((), ())), preferred_element_type=F32)
            c_ref[0, bb, h] = decay * c_h + upd
            n_ref[0, bb, h:h + 1, :] = decay * n_h + jnp.sum(kw, axis=0, keepdims=True)
            m_scr[bb * heads + h] = jnp.broadcast_to(m_new, m_scr.shape[1:])

            mu = jnp.mean(hh, axis=1, keepdims=True)
            xc = hh - mu
            var = jnp.mean(xc * xc, axis=1, keepdims=True)
            y = xc * lax.rsqrt(var + EPS) * nw_ref[:, sl] * _sigmoid(og_ref[bb, :, sl])
            om_ref[bb, :, sl] = (y * _silu(z_ref[bb, :, sl])).astype(om_ref.dtype)

    @pl.when(ci == nc - 1)
    def _():
        for s in range(c_ref.shape[0]):
            if s > 0:
                c_ref[s] = c_ref[0]
                n_ref[s] = n_ref[0]
            for bb in range(nb):
                tail_ref[s, bb] = ext_scr[bb, pad - (taps - 1):pad, :]
                for h in range(heads):
                    m_ref[s, bb, h:h + 1, :] = m_scr[bb * heads + h, 0:1, :]


def mlstm_branch(u, u_if, conv_w, conv_b, b_if, norm_w, c0, n0, m0, buf0, *, state_layer, layer, depth, prev,
                 chunk, nb, qk_col, v_col, og_col, z_col):
    b, t, _ = u.shape
    _, _, heads, dh, _ = c0.shape
    mw = heads * dh
    taps = conv_w.shape[0]
    assert chunk >= taps - 1 and t % chunk == 0 and b % nb == 0
    bif_pad = jnp.zeros((1, LANES), F32).at[0, :2 * heads].set(b_if)
    kern = functools.partial(_mlstm_kernel, n_prev=0 if prev is None else 4, heads=heads, taps=taps)
    const2 = lambda bi, ci: (0, 0)
    st_in = lambda bi, ci: (state_layer, bi, 0, 0)
    nslab, slab = _slabs(prev, layer, depth)
    st_out = lambda bi, ci: (slab, bi, 0, 0)
    res = _stacked_call(
        kern,
        grid=(b // nb, t // chunk),
        in_specs=[
            pl.BlockSpec((nb, chunk, 2 * mw), lambda bi, ci: (bi, ci, qk_col // (2 * mw))),
            pl.BlockSpec((nb, chunk, mw), lambda bi, ci: (bi, ci, v_col // mw)),
            pl.BlockSpec((nb, chunk, mw), lambda bi, ci: (bi, ci, og_col // mw)),
            pl.BlockSpec((nb, chunk, mw), lambda bi, ci: (bi, ci, z_col // mw)),
            pl.BlockSpec((nb, chunk, LANES), lambda bi, ci: (bi, ci, 0)),
            pl.BlockSpec((taps, 2 * mw), const2),
            pl.BlockSpec((1, 2 * mw), const2),
            pl.BlockSpec((1, LANES), const2),
            pl.BlockSpec((1, mw), const2),
            pl.BlockSpec((None, nb, heads, dh, dh), lambda bi, ci: (state_layer, bi, 0, 0, 0)),
            pl.BlockSpec((None, nb, heads, dh), st_in),
            pl.BlockSpec((None, nb, heads, LANES), st_in),
            pl.BlockSpec((None, nb, taps - 1, 2 * mw), st_in),
        ],
        inputs=[u, u, u, u, u_if, conv_w, conv_b.reshape(1, 2 * mw), bif_pad, norm_w.reshape(1, mw),
                c0, n0, m0, buf0],
        out_specs=[
            pl.BlockSpec((nb, chunk, mw), lambda bi, ci: (bi, ci, 0)),
            pl.BlockSpec((nslab, nb, heads, dh, dh), lambda bi, ci: (slab, bi, 0, 0, 0)),
            pl.BlockSpec((nslab, nb, heads, dh), st_out),
            pl.BlockSpec((nslab, nb, heads, LANES), st_out),
            pl.BlockSpec((nslab, nb, taps - 1, 2 * mw), st_out),
        ],
        out_shape=[
            jax.ShapeDtypeStruct((b, t, mw), BF16),
            jax.ShapeDtypeStruct((depth, b, heads, dh, dh), F32),
            jax.ShapeDtypeStruct((depth, b, heads, dh), F32),
            jax.ShapeDtypeStruct((depth, b, heads, LANES), F32),
            jax.ShapeDtypeStruct((depth, b, taps - 1, 2 * mw), F32),
        ],
        n_stacked=4, prev=prev,
        scratch_shapes=[pltpu.VMEM((nb, chunk + SUBLANES, 2 * mw), F32),
                        pltpu.VMEM((nb * heads, SUBLANES, LANES), F32)],
        compiler_params=_params(("parallel", "arbitrary")),
        name="mlstm_branch",
    )
    return res[0], tuple(res[1:])


def _wprep_kernel(w_hbm, o_ref, buf, sems, *, n_head_blk, skip):
    l, k = pl.program_id(0), pl.program_id(1)
    nk = pl.num_programs(1)
    tn = o_ref.shape[0]

    def copy(kk, slot):
        off = pl.multiple_of(kk * tn + jnp.where(kk >= n_head_blk, skip, 0), SUBLANES)
        return pltpu.make_async_copy(w_hbm.at[l, pl.ds(off, tn), :], buf.at[slot], sems.at[slot])

    slot = k % 2

    @pl.when(k == 0)
    def _():
        copy(k, slot).start()

    @pl.when(k + 1 < nk)
    def _():
        copy(k + 1, 1 - slot).start()

    copy(k, slot).wait()
    o_ref[...] = buf[slot].astype(o_ref.dtype)


def prepare_in_weight(w_t_all, *, head_rows, skip, tn):
    depth, p_in, d = w_t_all.shape
    n = p_in - skip
    assert head_rows % tn == 0 and n % tn == 0 and skip % SUBLANES == 0
    kern = functools.partial(_wprep_kernel, n_head_blk=head_rows // tn, skip=skip)
    return pl.pallas_call(
        kern,
        grid=(depth, n // tn),
        in_specs=[pl.BlockSpec(memory_space=pl.ANY)],
        out_specs=pl.BlockSpec((None, tn, d), lambda l, k: (l, k, 0)),
        out_shape=jax.ShapeDtypeStruct((depth, n, d), BF16),
        scratch_shapes=[pltpu.VMEM((2, tn, d), F32), pltpu.SemaphoreType.DMA((2,))],
        compiler_params=_params(("arbitrary", "arbitrary")),
        name="prepare_in_weight",
    )(w_t_all)


def _project_in(x, lw, *, layer, depth, prev, tm):
    b, t, d = x.shape
    col = lw["col"]
    return in_projection(
        x.reshape(b * t, d), lw["norm_pre"], lw["w_t"], lw["w_if"], layer=layer, depth=depth,
        heads=lw["heads"], k_col=col["ak"], v_col=col["av"], prev_kv=None if prev is None else prev[0],
        tm=tm, tn=lw["aw"])


def _finish_group(x, u2, uif2, o_a, lw, states, *, layer, depth, prev, chunk, nb, tm):
    b, t, d = x.shape
    aw, mw, col = lw["aw"], lw["mw"], lw["col"]
    c0, n0, m0, buf0, state_layer = states
    o_m, st = mlstm_branch(u2.reshape(b, t, -1), uif2.reshape(b, t, LANES), lw["conv_w"], lw["conv_b"], lw["b_if"],
                           lw["mlstm_norm_w"], c0, n0, m0, buf0, state_layer=state_layer, layer=layer,
                           depth=depth, prev=None if prev is None else prev[1], chunk=chunk, nb=nb,
                           qk_col=col["mq"], v_col=col["mv"], og_col=col["mo"], z_col=col["mz"])
    y = gated_merge(o_a.reshape(b * t, aw), o_m.reshape(b * t, mw), lw["w_pa"], lw["w_pm"], u2,
                    col["ga"], col["gm"], tm=tm, tn=1024)
    x_new = out_projection(y, lw["w_out"], lw["norm_post"], x.reshape(b * t, d), tm=min(tm, 512)).reshape(b, t, d)
    return x_new, st


def kernel(x_prompt, x_sample, cache_k, cache_v, state_C, state_n, state_m, state_conv, page_table,
           norm_pre, norm_post, w_in, b_if, conv_w, conv_b, lambda_qk, attn_norm_w, mlstm_norm_w,
           w_pa, w_pm, w_out):
    depth = w_in.shape[0]
    d_model = x_prompt.shape[-1]
    heads = cache_k.shape[3]
    dqk = lambda_qk.shape[-1]
    dh = 2 * dqk
    aw = heads * dh
    m_heads, m_dh = state_C.shape[2], state_C.shape[3]
    mw = m_heads * m_dh
    bp, sp, _ = x_prompt.shape
    bs, ts, _ = x_sample.shape
    taps = conv_w.shape[1]
    n_if = 2 * m_heads
    if_col = 4 * aw + 4 * mw
    col = {"aq": 0, "ak": aw, "av": 2 * aw, "az": 3 * aw, "mq": 4 * aw, "mv": 4 * aw + 2 * mw,
           "mo": 4 * aw + 3 * mw, "mz": if_col, "ga": if_col + mw, "gm": if_col + mw + d_model}

    zero_states = (jnp.zeros((1, bp, m_heads, m_dh, m_dh), F32), jnp.zeros((1, bp, m_heads, m_dh), F32),
                   jnp.zeros((1, bp, m_heads, LANES), F32), jnp.zeros((1, bp, taps - 1, 2 * mw), F32))
    m_rep = jnp.broadcast_to(state_m[..., None], state_m.shape + (LANES,))

    w_t_all = prepare_in_weight(jnp.swapaxes(w_in, 1, 2), head_rows=if_col, skip=n_if, tn=aw)
    tq = 512
    tm_p, tm_s = min(1024, bp * sp), bs * ts
    fuse = bp * heads * (sp // tq) == bs

    xp, xs = x_prompt, x_sample
    prev_p = prev_s = None
    for l in range(depth):
        lam_init = 0.8 - 0.6 * math.exp(-0.3 * l)
        lq = lambda_qk[l].astype(F32)
        lam = jnp.exp(jnp.sum(lq[0] * lq[1])) - jnp.exp(jnp.sum(lq[2] * lq[3])) + lam_init
        scal = jnp.reshape(lam, (1,)).astype(F32)
        w_gates = jnp.swapaxes(w_in[l, :, if_col:if_col + n_if], 0, 1)
        lw = dict(heads=heads, aw=aw, mw=mw, col=col, norm_pre=norm_pre[l], norm_post=norm_post[l],
                  w_t=w_t_all, w_if=jnp.pad(w_gates, ((0, LANES - n_if), (0, 0))),
                  conv_w=conv_w[l], conv_b=conv_b[l], b_if=b_if[l], mlstm_norm_w=mlstm_norm_w[l],
                  w_pa=w_pa[l].astype(BF16), w_pm=w_pm[l].astype(BF16), w_out=w_out[l].astype(BF16))
        attn_cols = dict(dqk=dqk, q_col=col["aq"], z_col=col["az"], post_scale=1.0 - lam_init)

        u2p, uifp, kp_st, vp_st = _project_in(xp, lw, layer=l, depth=depth, prev=prev_p, tm=tm_p)
        u2s, uifs, ks_st, vs_st = _project_in(xs, lw, layer=l, depth=depth, prev=prev_s, tm=tm_s)
        u_p, u_s = u2p.reshape(bp, sp, -1), u2s.reshape(bs, ts, -1)
        if fuse:
            oa_p, oa_s = fused_attention(u_p, u_s, ks_st, vs_st, l, cache_k, cache_v, l, page_table, scal,
                                         attn_norm_w[l], k_col=col["ak"], v_col=col["av"], tq=tq, **attn_cols)
        else:
            oa_p = prompt_attention(u_p, scal, attn_norm_w[l], heads=heads, k_col=col["ak"], v_col=col["av"],
                                    tq=tq, **attn_cols)
            oa_s = sample_attention(u_s, ks_st, vs_st, l, cache_k, cache_v, l, page_table, scal, attn_norm_w[l],
                                    **attn_cols)
        xp, st_p = _finish_group(xp, u2p, uifp, oa_p, lw, zero_states + (0,), layer=l, depth=depth, prev=prev_p,
                                 chunk=CHUNK if sp % CHUNK == 0 else sp, nb=1, tm=tm_p)
        xs, st_s = _finish_group(xs, u2s, uifs, oa_s, lw, (state_C, state_n, m_rep, state_conv, l), layer=l,
                                 depth=depth, prev=prev_s, chunk=CHUNK if ts % CHUNK == 0 else ts, nb=4, tm=tm_s)
        prev_p, prev_s = ((kp_st, vp_st), st_p), ((ks_st, vs_st), st_s)

    def unpack(prev, b, t):
        (k_st, v_st), (c, n, m, tail) = prev
        return (k_st.reshape(depth, b, t, heads, dh), v_st.reshape(depth, b, t, heads, dh), c, n, m[..., 0], tail)

    return (xp, xs) + unpack(prev_p, bp, sp) + unpack(prev_s, bs, ts)
```

```python
import functools
import math

import jax
import jax.numpy as jnp
import numpy as np
from jax import lax
from jax.experimental import pallas as pl
from jax.experimental.pallas import tpu as pltpu

F32 = jnp.float32
BF16 = jnp.bfloat16

EPS = 1e-6
CHUNK = 64
LANES = 128
SUBLANES = 8
BF16_ROWS = 16
NEG_BIG = -1e30
VMEM_LIMIT = 56 * 1024 * 1024
LOG2E = math.log2(math.e)
KV_GROUP = 2


def _sigmoid(x):
    return 1.0 / (1.0 + jnp.exp(-x))


def _silu(x):
    return x * _sigmoid(x)


def _params(sem):
    return pltpu.CompilerParams(dimension_semantics=sem, vmem_limit_bytes=VMEM_LIMIT)


def _slabs(prev, layer, depth):
    return (depth, 0) if prev is None else (1, layer)


def _stacked_call(kernel_fn, *, grid, in_specs, inputs, out_specs, out_shape, n_stacked, prev, **kw):
    aliases = {}
    if prev is not None:
        first = len(out_shape) - n_stacked
        for k, p in enumerate(prev):
            aliases[len(inputs) + k] = first + k
        in_specs = list(in_specs) + [pl.BlockSpec(memory_space=pl.ANY)] * len(prev)
        inputs = list(inputs) + list(prev)
    return pl.pallas_call(kernel_fn, grid=grid, in_specs=in_specs, out_specs=out_specs, out_shape=out_shape,
                          input_output_aliases=aliases, **kw)(*inputs)


def _dot_nt(a, b):
    return lax.dot_general(a, b, (((1,), (1,)), ((), ())), preferred_element_type=F32)


def _inproj_kernel(x_ref, nw_ref, w_ref, wif_ref, *rest, n_prev, k_blk, v_blk, heads, nslab, slab):
    u_ref, uif_ref, k_hbm, v_hbm, h_scr, k_stage, v_stage, sems = rest[n_prev:]
    i, j = pl.program_id(0), pl.program_id(1)
    tm = x_ref.shape[0]
    dh = k_stage.shape[-1]
    rows = tm * heads
    row0 = pl.multiple_of(i * rows, rows)
    targets = ((k_blk, k_stage, k_hbm), (v_blk, v_stage, v_hbm))

    def copies(which):
        _, stage, hbm = targets[which]
        return [pltpu.make_async_copy(stage, hbm.at[slab + s, pl.ds(row0, rows), :], sems.at[which, s])
                for s in range(nslab)]

    @pl.when(j == 0)
    def _():
        x = x_ref[...]
        ms = jnp.mean(x * x, axis=-1, keepdims=True)
        h = (x * lax.rsqrt(ms + EPS) * nw_ref[...]).astype(BF16)
        h_scr[...] = h
        uif_ref[...] = _dot_nt(h, wif_ref[...].astype(BF16))

    acc = _dot_nt(h_scr[...], w_ref[...])
    u_ref[...] = acc

    for which, (blk, stage, _) in enumerate(targets):
        @pl.when(j == blk)
        def _():
            for h in range(heads):
                stage[pl.ds(h, tm, stride=heads), :] = acc[:, h * dh:(h + 1) * dh]
            for c in copies(which):
                c.start()

    @pl.when(j == pl.num_programs(1) - 1)
    def _():
        for which in range(len(targets)):
            for c in copies(which):
                c.wait()


def in_projection(x, norm_w, w_t, w_if, *, layer, depth, heads, k_col, v_col, prev_kv, tm, tn):
    t, d = x.shape
    n = w_t.shape[1]
    dh = tn // heads
    k_blk, v_blk = k_col // tn, v_col // tn
    assert k_col % tn == 0 and v_col % tn == 0 and tn == heads * dh and max(k_blk, v_blk) < n // tn - 1
    nslab, slab = _slabs(prev_kv, layer, depth)
    kern = functools.partial(_inproj_kernel, n_prev=0 if prev_kv is None else 2, k_blk=k_blk, v_blk=v_blk,
                             heads=heads, nslab=nslab, slab=slab)
    kv_spec = pl.BlockSpec(memory_space=pl.ANY)
    kv_shape = jax.ShapeDtypeStruct((depth, t * heads, dh), F32)
    stage = pltpu.VMEM((tm * heads, dh), F32)
    return _stacked_call(
        kern,
        grid=(t // tm, n // tn),
        in_specs=[
            pl.BlockSpec((tm, d), lambda i, j: (i, 0)),
            pl.BlockSpec((1, d), lambda i, j: (0, 0)),
            pl.BlockSpec((None, tn, d), lambda i, j: (layer, j, 0)),
            pl.BlockSpec((LANES, d), lambda i, j: (0, 0)),
        ],
        inputs=[x, norm_w.reshape(1, d), w_t, w_if],
        out_specs=[
            pl.BlockSpec((tm, tn), lambda i, j: (i, j)),
            pl.BlockSpec((tm, LANES), lambda i, j: (i, 0)),
            kv_spec, kv_spec,
        ],
        out_shape=[jax.ShapeDtypeStruct((t, n), F32), jax.ShapeDtypeStruct((t, LANES), F32), kv_shape, kv_shape],
        n_stacked=2, prev=prev_kv,
        scratch_shapes=[pltpu.VMEM((tm, d), BF16), stage, stage, pltpu.SemaphoreType.DMA((2, nslab))],
        compiler_params=_params(("arbitrary", "arbitrary")),
        name="in_projection",
    )


def _merge_kernel(oa_ref, om_ref, *refs, n_col):
    y_ref = refs[4 * n_col]
    tn = refs[0].shape[1]
    for c in range(n_col):
        wpa_ref, wpm_ref, ga_ref, gm_ref = refs[4 * c:4 * c + 4]
        pa = jnp.dot(oa_ref[...], wpa_ref[...], preferred_element_type=F32)
        pm = jnp.dot(om_ref[...], wpm_ref[...], preferred_element_type=F32)
        y = _sigmoid(ga_ref[...]) * pa + _sigmoid(gm_ref[...]) * pm
        y_ref[:, c * tn:(c + 1) * tn] = y.astype(y_ref.dtype)


def gated_merge(o_a, o_m, w_pa, w_pm, u, ga_col, gm_col, *, tm, tn):
    t, wa = o_a.shape
    wm = o_m.shape[1]
    d = w_pa.shape[1]
    assert ga_col % tn == 0 and gm_col % tn == 0 and d % tn == 0
    n_col = d // tn
    specs, args = [], []
    for c in range(n_col):
        specs += [pl.BlockSpec((wa, tn), lambda i, c=c: (0, c)),
                  pl.BlockSpec((wm, tn), lambda i, c=c: (0, c)),
                  pl.BlockSpec((tm, tn), lambda i, c=c: (i, ga_col // tn + c)),
                  pl.BlockSpec((tm, tn), lambda i, c=c: (i, gm_col // tn + c))]
        args += [w_pa, w_pm, u, u]
    return pl.pallas_call(
        functools.partial(_merge_kernel, n_col=n_col),
        grid=(t // tm,),
        in_specs=[pl.BlockSpec((tm, wa), lambda i: (i, 0)), pl.BlockSpec((tm, wm), lambda i: (i, 0))] + specs,
        out_specs=pl.BlockSpec((tm, d), lambda i: (i, 0)),
        out_shape=jax.ShapeDtypeStruct((t, d), BF16),
        compiler_params=_params(("parallel",)),
        name="gated_merge",
    )(o_a, o_m, *args)


def _outproj_kernel(y_ref, w_ref, nw_ref, x_ref, o_ref):
    out = jnp.dot(y_ref[...], w_ref[...], preferred_element_type=F32)
    ms = jnp.mean(out * out, axis=-1, keepdims=True)
    o_ref[...] = x_ref[...] + out * lax.rsqrt(ms + EPS) * nw_ref[...]


def out_projection(y, w_out, norm_w, x, *, tm):
    t, d = x.shape
    return pl.pallas_call(
        _outproj_kernel,
        grid=(t // tm,),
        in_specs=[
            pl.BlockSpec((tm, d), lambda i: (i, 0)),
            pl.BlockSpec((d, d), lambda i: (0, 0)),
            pl.BlockSpec((1, d), lambda i: (0, 0)),
            pl.BlockSpec((tm, d), lambda i: (i, 0)),
        ],
        out_specs=pl.BlockSpec((tm, d), lambda i: (i, 0)),
        out_shape=jax.ShapeDtypeStruct((t, d), F32),
        compiler_params=_params(("parallel",)),
        name="out_projection",
    )(y, w_out, norm_w.reshape(1, d), x)


def _prompt_attn_kernel(*refs, **kw):
    _prompt_attn_body(pl.program_id(2), *refs, **kw)


def _prompt_attn_body(qi, scal_ref, q_ref, k_ref, v_ref, z_ref, nw_ref, o_ref, kb_scr, vt_scr,
                      *, tq, dqk, scale_log2, post_scale):
    lam = scal_ref[0]
    dv = v_ref.shape[-1]
    n_blk = vt_scr.shape[0]

    @pl.when(qi == 0)
    def _():
        kb_scr[...] = k_ref[0].astype(BF16)
        for c in range(n_blk):
            vt_scr[c, 0:dv, :] = jnp.transpose(v_ref[0, c * tq:(c + 1) * tq, :]).astype(BF16)
            vt_scr[c, dv:, :] = jnp.ones((BF16_ROWS, tq), BF16)

    q = q_ref[0] * scale_log2
    lane = lax.broadcasted_iota(jnp.int32, q.shape, 1)
    q1 = jnp.where(lane < dqk, q, 0.0).astype(BF16)
    q2 = jnp.where(lane >= dqk, q, 0.0).astype(BF16)

    def scores_t(k, qm):
        return lax.dot_general(k, qm, (((1,), (1,)), ((), ())), preferred_element_type=F32)

    def online(state, s_t, vt):
        m, acc = state
        m_new = jnp.maximum(m, jnp.max(s_t, axis=0, keepdims=True))
        alpha = jnp.exp2(m - m_new)
        p = jnp.exp2(s_t - m_new)
        acc = alpha * acc + jnp.dot(vt, p.astype(BF16), preferred_element_type=F32)
        return m_new, acc

    def scores_of(j):
        k = kb_scr[pl.ds(pl.multiple_of(j * tq, tq), tq), :]
        return scores_t(k, q1), scores_t(k, q2)

    def causal(s):
        krow = lax.broadcasted_iota(jnp.int32, s.shape, 0)
        qcol = lax.broadcasted_iota(jnp.int32, s.shape, 1)
        return jnp.where(krow <= qcol, s, -jnp.inf)

    def group(j, carry, n, last_is_diagonal):
        st1, st2 = carry
        scores = [scores_of(j + i) for i in range(n)]
        for i, (s1, s2) in enumerate(scores):
            if last_is_diagonal and i == n - 1:
                s1, s2 = causal(s1), causal(s2)
            vt = vt_scr[j + i]
            st1, st2 = online(st1, s1, vt), online(st2, s2, vt)
        return st1, st2

    def init():
        return jnp.full((1, tq), NEG_BIG, F32), jnp.zeros((dv + BF16_ROWS, tq), F32)

    n_full = qi // KV_GROUP
    carry = lax.fori_loop(0, n_full, lambda g, c: group(KV_GROUP * g, c, KV_GROUP, False), (init(), init()))
    tails = [functools.partial(group, n=r + 1, last_is_diagonal=True) for r in range(KV_GROUP)]
    rest0 = n_full * KV_GROUP
    (_, a1), (_, a2) = lax.switch(qi - rest0, [lambda c, f=f: f(rest0, c) for f in tails], carry)

    out_t = a1[0:dv] / a1[dv:dv + 1] - lam * (a2[0:dv] / a2[dv:dv + 1])
    out = jnp.transpose(out_t)
    ms = jnp.mean(out * out, axis=-1, keepdims=True)
    y = out * lax.rsqrt(ms + EPS) * nw_ref[...] * post_scale
    o_ref[0] = (y * _silu(z_ref[0])).astype(o_ref.dtype)


def prompt_attention(u, scal, norm_w, *, heads, dqk, q_col, k_col, v_col, z_col, post_scale, tq):
    b, s, _ = u.shape
    dh = 2 * dqk
    qb, kb, vb, zb = q_col // dh, k_col // dh, v_col // dh, z_col // dh
    kern = functools.partial(_prompt_attn_kernel, tq=tq, dqk=dqk, scale_log2=dqk ** -0.5 * LOG2E,
                             post_scale=post_scale)
    return pl.pallas_call(
        kern,
        grid=(b, heads, s // tq),
        in_specs=[
            pl.BlockSpec(memory_space=pltpu.SMEM),
            pl.BlockSpec((1, tq, dh), lambda bi, h, i: (bi, i, qb + h)),
            pl.BlockSpec((1, s, dh), lambda bi, h, i: (bi, 0, kb + h)),
            pl.BlockSpec((1, s, dh), lambda bi, h, i: (bi, 0, vb + h)),
            pl.BlockSpec((1, tq, dh), lambda bi, h, i: (bi, i, zb + h)),
            pl.BlockSpec((1, dh), lambda bi, h, i: (0, h)),
        ],
        out_specs=pl.BlockSpec((1, tq, dh), lambda bi, h, i: (bi, i, h)),
        out_shape=jax.ShapeDtypeStruct((b, s, heads * dh), BF16),
        scratch_shapes=[pltpu.VMEM((s, dh), BF16), pltpu.VMEM((s // tq, dh + BF16_ROWS, tq), BF16)],
        compiler_params=_params(("parallel", "parallel", "arbitrary")),
        name="prompt_attention",
    )(scal, u, u, u, u, norm_w.reshape(1, heads * dh))


def _sample_attn_kernel(pt_ref, scal_ref, q_ref, kn_ref, vn_ref, z_ref, nw_ref, bias_ref, biasn_ref, *rest,
                        n_pages, **kw):
    del pt_ref
    _sample_attn_body(scal_ref, q_ref, kn_ref, vn_ref, z_ref, nw_ref, bias_ref, biasn_ref,
                      rest[:n_pages], rest[n_pages:2 * n_pages], rest[2 * n_pages], rest[2 * n_pages + 1], **kw)


def _sample_attn_body(scal_ref, q_ref, kn_ref, vn_ref, z_ref, nw_ref, bias_ref, biasn_ref, k_refs, v_refs,
                      o_ref, s_scr, *, heads, dqk, nq, scale, post_scale):
    n_pages = len(k_refs)
    lam = scal_ref[0]
    dh = 2 * dqk
    rows_pp = k_refs[0].shape[0] * k_refs[0].shape[1]

    q = q_ref[0] * scale
    lane = lax.broadcasted_iota(jnp.int32, (nq, dh), 1)
    pieces = []
    for h in range(heads):
        qh = q[:, h * dh:(h + 1) * dh]
        pieces.append(jnp.where(lane < dqk, qh, 0.0))
        pieces.append(jnp.where(lane >= dqk, qh, 0.0))
    qall = jnp.concatenate(pieces, axis=0).astype(BF16)
    nrow = qall.shape[0]

    def scores(k2d):
        return lax.dot_general(qall, k2d, (((1,), (1,)), ((), ())), preferred_element_type=F32)

    bias = bias_ref[...]
    slabs = [slice(c * LANES, (c + 1) * LANES) for c in range(rows_pp // LANES)]
    m_acc = jnp.full((nrow, LANES), NEG_BIG, F32)
    for j in range(n_pages):
        kj = k_refs[j][...].reshape(rows_pp, dh).astype(BF16)
        s = scores(kj)
        for c in slabs:
            sc = s[:, c] + bias
            s_scr[:, j * rows_pp + c.start:j * rows_pp + c.stop] = sc
            m_acc = jnp.maximum(m_acc, sc)
    sn = scores(kn_ref[...].astype(BF16)) + biasn_ref[...]
    m = jnp.maximum(jnp.max(m_acc, axis=-1, keepdims=True), jnp.max(sn, axis=-1, keepdims=True))

    l_acc = jnp.zeros((nrow, LANES), F32)
    acc = jnp.zeros((nrow, dh), F32)
    for j in range(n_pages):
        p = jnp.exp(s_scr[:, j * rows_pp:(j + 1) * rows_pp] - m)
        for c in slabs:
            l_acc = l_acc + p[:, c]
        vj = v_refs[j][...].reshape(rows_pp, dh).astype(BF16)
        acc = acc + jnp.dot(p.astype(BF16), vj, preferred_element_type=F32)
    pn = jnp.exp(sn - m)
    l = jnp.sum(l_acc, axis=-1, keepdims=True) + jnp.sum(pn, axis=-1, keepdims=True)
    acc = acc + jnp.dot(pn.astype(BF16), vn_ref[...].astype(BF16), preferred_element_type=F32)

    o = acc / l
    z = z_ref[0]
    nw = nw_ref[...]
    for h in range(heads):
        r0 = h * 2 * nq
        oh = o[r0:r0 + nq] - lam * o[r0 + nq:r0 + 2 * nq]
        ms = jnp.mean(oh * oh, axis=-1, keepdims=True)
        y = oh * lax.rsqrt(ms + EPS) * nw[:, h * dh:(h + 1) * dh] * post_scale
        o_ref[0, :, h * dh:(h + 1) * dh] = (y * _silu(z[:, h * dh:(h + 1) * dh])).astype(o_ref.dtype)


def _sample_masks(heads, nq, page_rows):
    r = np.arange(heads * 2 * nq)
    rh, rt = r // (2 * nq), r % nq
    c = np.arange(page_rows)
    past = np.where((c[None, :] % heads) == rh[:, None], 0.0, -np.inf).astype(np.float32)
    cn = np.arange(nq * heads)
    ok = ((cn[None, :] % heads) == rh[:, None]) & ((cn[None, :] // heads) <= rt[:, None])
    new = np.where(ok, 0.0, -np.inf).astype(np.float32)
    return jnp.asarray(past), jnp.asarray(new)


def sample_attention(u, k_rows, v_rows, kv_layer, cache_k, cache_v, layer, page_table, scal, norm_w,
                     *, dqk, q_col, z_col, post_scale):
    b, nq, _ = u.shape
    _, _, page, heads, dh = cache_k.shape
    n_pages = page_table.shape[1]
    width = heads * dh
    qb, zb = q_col // width, z_col // width
    assert LANES % heads == 0 and (page * heads) % LANES == 0
    bias, bias_new = _sample_masks(heads, nq, LANES)
    nrow = heads * 2 * nq

    def page_spec(j):
        return pl.BlockSpec((None, None, page, heads, dh), lambda i, pt: (layer, pt[i, j], 0, 0, 0))

    new_spec = pl.BlockSpec((None, nq * heads, dh), lambda i, pt: (kv_layer, i, 0))
    kern = functools.partial(_sample_attn_kernel, n_pages=n_pages, heads=heads, dqk=dqk, nq=nq,
                             scale=dqk ** -0.5, post_scale=post_scale)
    grid_spec = pltpu.PrefetchScalarGridSpec(
        num_scalar_prefetch=1,
        grid=(b,),
        in_specs=[
            pl.BlockSpec(memory_space=pltpu.SMEM),
            pl.BlockSpec((1, nq, width), lambda i, pt: (i, 0, qb)),
            new_spec, new_spec,
            pl.BlockSpec((1, nq, width), lambda i, pt: (i, 0, zb)),
            pl.BlockSpec((1, width), lambda i, pt: (0, 0)),
            pl.BlockSpec(bias.shape, lambda i, pt: (0, 0)),
            pl.BlockSpec(bias_new.shape, lambda i, pt: (0, 0)),
        ] + [page_spec(j) for j in range(n_pages)] * 2,
        out_specs=pl.BlockSpec((1, nq, width), lambda i, pt: (i, 0, 0)),
        scratch_shapes=[pltpu.VMEM((nrow, n_pages * page * heads), F32)],
    )
    return pl.pallas_call(
        kern,
        grid_spec=grid_spec,
        out_shape=jax.ShapeDtypeStruct((b, nq, width), BF16),
        compiler_params=_params(("arbitrary",)),
        name="sample_attention",
    )(page_table, scal, u, k_rows, v_rows, u, norm_w.reshape(1, width), bias, bias_new,
      *([cache_k] * n_pages), *([cache_v] * n_pages))


def _fused_attn_kernel(pt_ref, scal_ref, qp_ref, kp_ref, vp_ref, zp_ref, nwp_ref,
                       qs_ref, kn_ref, vn_ref, zs_ref, nws_ref, bias_ref, biasn_ref, *rest,
                       n_pages, n_qblk, heads, dqk, nq, tq, post_scale):
    del pt_ref
    k_refs, v_refs = rest[:n_pages], rest[n_pages:2 * n_pages]
    op_ref, os_ref, kb_scr, vt_scr, s_scr = rest[2 * n_pages:]
    _sample_attn_body(scal_ref, qs_ref, kn_ref, vn_ref, zs_ref, nws_ref, bias_ref, biasn_ref, k_refs, v_refs,
                      os_ref, s_scr, heads=heads, dqk=dqk, nq=nq, scale=dqk ** -0.5, post_scale=post_scale)
    _prompt_attn_body(pl.program_id(0) % n_qblk, scal_ref, qp_ref, kp_ref, vp_ref, zp_ref, nwp_ref, op_ref,
                      kb_scr, vt_scr, tq=tq, dqk=dqk, scale_log2=dqk ** -0.5 * LOG2E, post_scale=post_scale)


def fused_attention(u_p, u_s, k_rows, v_rows, kv_layer, cache_k, cache_v, layer, page_table, scal, norm_w,
                    *, dqk, q_col, k_col, v_col, z_col, post_scale, tq):
    bp, sp, _ = u_p.shape
    bs, nq, _ = u_s.shape
    _, _, page, heads, dh = cache_k.shape
    n_pages = page_table.shape[1]
    width = heads * dh
    n_qblk = sp // tq
    assert bp * heads * n_qblk == bs
    qb, kb, vb, zb = q_col // dh, k_col // dh, v_col // dh, z_col // dh
    qbs, zbs = q_col // width, z_col // width
    assert LANES % heads == 0 and (page * heads) % LANES == 0
    bias, bias_new = _sample_masks(heads, nq, LANES)

    def pb(s):
        return s // (heads * n_qblk)

    def ph(s):
        return (s // n_qblk) % heads

    def pq(s):
        return s % n_qblk

    def page_spec(j):
        return pl.BlockSpec((None, None, page, heads, dh), lambda s, pt: (layer, pt[s, j], 0, 0, 0))

    new_spec = pl.BlockSpec((None, nq * heads, dh), lambda s, pt: (kv_layer, s, 0))
    kern = functools.partial(_fused_attn_kernel, n_pages=n_pages, n_qblk=n_qblk, heads=heads, dqk=dqk, nq=nq,
                             tq=tq, post_scale=post_scale)
    grid_spec = pltpu.PrefetchScalarGridSpec(
        num_scalar_prefetch=1,
        grid=(bs,),
        in_specs=[
            pl.BlockSpec(memory_space=pltpu.SMEM),
            pl.BlockSpec((1, tq, dh), lambda s, pt: (pb(s), pq(s), qb + ph(s))),
            pl.BlockSpec((1, sp, dh), lambda s, pt: (pb(s), 0, kb + ph(s))),
            pl.BlockSpec((1, sp, dh), lambda s, pt: (pb(s), 0, vb + ph(s))),
            pl.BlockSpec((1, tq, dh), lambda s, pt: (pb(s), pq(s), zb + ph(s))),
            pl.BlockSpec((1, dh), lambda s, pt: (0, ph(s))),
            pl.BlockSpec((1, nq, width), lambda s, pt: (s, 0, qbs)),
            new_spec, new_spec,
            pl.BlockSpec((1, nq, width), lambda s, pt: (s, 0, zbs)),
            pl.BlockSpec((1, width), lambda s, pt: (0, 0)),
            pl.BlockSpec(bias.shape, lambda s, pt: (0, 0)),
            pl.BlockSpec(bias_new.shape, lambda s, pt: (0, 0)),
        ] + [page_spec(j) for j in range(n_pages)] * 2,
        out_specs=[
            pl.BlockSpec((1, tq, dh), lambda s, pt: (pb(s), pq(s), ph(s))),
            pl.BlockSpec((1, nq, width), lambda s, pt: (s, 0, 0)),
        ],
        scratch_shapes=[pltpu.VMEM((sp, dh), BF16), pltpu.VMEM((n_qblk, dh + BF16_ROWS, tq), BF16),
                        pltpu.VMEM((heads * 2 * nq, n_pages * page * heads), F32)],
    )
    nw2 = norm_w.reshape(1, width)
    return pl.pallas_call(
        kern,
        grid_spec=grid_spec,
        out_shape=[jax.ShapeDtypeStruct((bp, sp, width), BF16), jax.ShapeDtypeStruct((bs, nq, width), BF16)],
        compiler_params=_params(("arbitrary",)),
        name="fused_attention",
    )(page_table, scal, u_p, u_p, u_p, u_p, nw2, u_s, k_rows, v_rows, u_s, nw2, bias, bias_new,
      *([cache_k] * n_pages), *([cache_v] * n_pages))


def _mlstm_kernel(qk_ref, v_ref, og_ref, z_ref, if_ref, cw_ref, cb_ref, bif_ref, nw_ref,
                  c0_ref, n0_ref, m0_ref, buf0_ref, *rest, n_prev, heads, taps, single_chunk):
    om_ref, c_ref, n_ref, m_ref, tail_ref, ext_scr, m_scr = rest[n_prev:]
    ci = pl.program_id(1)
    nc = pl.num_programs(1)
    nb, L = qk_ref.shape[0], qk_ref.shape[1]
    mw = v_ref.shape[2]
    dh = mw // heads
    pad = SUBLANES
    nslab = c_ref.shape[0]

    if single_chunk:
        for bb in range(nb):
            ext_scr[bb, pad - (taps - 1):pad, :] = buf0_ref[bb]
    else:
        @pl.when(ci == 0)
        def _():
            c_ref[0] = c0_ref[...]
            n_ref[0] = n0_ref[...]
            for bb in range(nb):
                for h in range(heads):
                    m_scr[bb * heads + h] = jnp.broadcast_to(m0_ref[bb, h:h + 1, :], m_scr.shape[1:])
                ext_scr[bb, pad - (taps - 1):pad, :] = buf0_ref[bb]

    row = lax.broadcasted_iota(jnp.int32, (L, L), 0)
    col = lax.broadcasted_iota(jnp.int32, (L, L), 1)
    eye = row == col
    low = col <= row

    for bb in range(nb):
        ext_scr[bb, pad:pad + L, :] = qk_ref[bb]
        conv = cb_ref[...]
        for j in range(taps):
            r0 = pad - (taps - 1) + j
            conv = conv + ext_scr[bb, r0:r0 + L, :] * cw_ref[j:j + 1, :]

        ext_scr[bb, 0:pad, :] = ext_scr[bb, L:L + pad, :]
        qk = _silu(conv)
        q_all = qk[:, :mw]
        k_all = qk[:, mw:] * (dh ** -0.5)
        v_all = v_ref[bb]
        g = if_ref[bb] + bif_ref[...]
        lf_all = jnp.minimum(g, 0.0) - jnp.log(1.0 + jnp.exp(-jnp.abs(g)))
        for h in range(heads):
            li_col = g[:, h:h + 1]
            lf_col = lf_all[:, heads + h:heads + h + 1]
            lf_row = jnp.sum(jnp.where(eye, lf_col, 0.0), axis=0, keepdims=True)
            bt_col = jnp.sum(jnp.where(low, lf_row, 0.0), axis=1, keepdims=True)
            r_row = jnp.sum(jnp.where(eye, li_col - bt_col, 0.0), axis=0, keepdims=True)
            m_prev = m0_ref[bb, h:h + 1, 0:1] if single_chunk else m_scr[bb * heads + h, 0:1, 0:1]

            log_d = jnp.where(low, bt_col + r_row, -jnp.inf)
            log_inter = bt_col + m_prev
            m_t = jnp.maximum(log_inter, jnp.max(log_d, axis=1, keepdims=True))
            d_mat = jnp.exp(log_d - m_t)
            inter = jnp.exp(log_inter - m_t)

            sl = slice(h * dh, (h + 1) * dh)
            q_h, k_h = q_all[:, sl], k_all[:, sl]
            q_b, k_b, v_b = q_h.astype(BF16), k_h.astype(BF16), v_all[:, sl].astype(BF16)
            c_h = c0_ref[bb, h] if single_chunk else c_ref[0, bb, h]
            n_h = n0_ref[bb, h:h + 1, :] if single_chunk else n_ref[0, bb, h:h + 1, :]

            s = lax.dot_general(q_b, k_b, (((1,), (1,)), ((), ())), preferred_element_type=F32) * d_mat
            num = (jnp.dot(s.astype(BF16), v_b, preferred_element_type=F32)
                   + jnp.dot(q_b, c_h.astype(BF16), preferred_element_type=F32) * inter)
            den = jnp.sum(s, axis=1, keepdims=True) + inter * jnp.sum(q_h * n_h, axis=1, keepdims=True)
            den = jnp.maximum(jnp.abs(den), jnp.exp(-m_t))
            hh = num / den

            b_last = jnp.sum(lf_col, axis=0, keepdims=True)
            log_w = b_last - bt_col + li_col
            m_new = jnp.maximum(b_last + m_prev, jnp.max(log_w, axis=0, keepdims=True))
            wk = jnp.exp(log_w - m_new)
            decay = jnp.exp(b_last + m_prev - m_new)
            kw = k_h * wk
            upd = lax.dot_general(kw.astype(BF16), v_b, (((0,), (0,)), ((), ())), preferred_element_type=F32)
            c_new = decay * c_h + upd
            n_new = decay * n_h + jnp.sum(kw, axis=0, keepdims=True)
            for s in range(nslab if single_chunk else 1):
                c_ref[s, bb, h] = c_new
                n_ref[s, bb, h:h + 1, :] = n_new
            m_scr[bb * heads + h] = jnp.broadcast_to(m_new, m_scr.shape[1:])

            mu = jnp.mean(hh, axis=1, keepdims=True)
            xc = hh - mu
            var = jnp.mean(xc * xc, axis=1, keepdims=True)
            y = xc * lax.rsqrt(var + EPS) * nw_ref[:, sl] * _sigmoid(og_ref[bb, :, sl])
            om_ref[bb, :, sl] = (y * _silu(z_ref[bb, :, sl])).astype(om_ref.dtype)

    def finish():
        for s in range(nslab):
            if s > 0 and not single_chunk:
                c_ref[s] = c_ref[0]
                n_ref[s] = n_ref[0]
            for bb in range(nb):
                tail_ref[s, bb] = ext_scr[bb, pad - (taps - 1):pad, :]
                for h in range(heads):
                    m_ref[s, bb, h:h + 1, :] = m_scr[bb * heads + h, 0:1, :]

    if single_chunk:
        finish()
    else:
        pl.when(ci == nc - 1)(finish)


def mlstm_branch(u, u_if, conv_w, conv_b, b_if, norm_w, c0, n0, m0, buf0, *, state_layer, layer, depth, prev,
                 chunk, nb, qk_col, v_col, og_col, z_col):
    b, t, _ = u.shape
    _, _, heads, dh, _ = c0.shape
    mw = heads * dh
    taps = conv_w.shape[0]
    assert chunk >= taps - 1 and t % chunk == 0 and b % nb == 0
    bif_pad = jnp.zeros((1, LANES), F32).at[0, :2 * heads].set(b_if)
    kern = functools.partial(_mlstm_kernel, n_prev=0 if prev is None else 4, heads=heads, taps=taps,
                             single_chunk=(t == chunk))
    const2 = lambda bi, ci: (0, 0)
    st_in = lambda bi, ci: (state_layer, bi, 0, 0)
    nslab, slab = _slabs(prev, layer, depth)
    st_out = lambda bi, ci: (slab, bi, 0, 0)
    res = _stacked_call(
        kern,
        grid=(b // nb, t // chunk),
        in_specs=[
            pl.BlockSpec((nb, chunk, 2 * mw), lambda bi, ci: (bi, ci, qk_col // (2 * mw))),
            pl.BlockSpec((nb, chunk, mw), lambda bi, ci: (bi, ci, v_col // mw)),
            pl.BlockSpec((nb, chunk, mw), lambda bi, ci: (bi, ci, og_col // mw)),
            pl.BlockSpec((nb, chunk, mw), lambda bi, ci: (bi, ci, z_col // mw)),
            pl.BlockSpec((nb, chunk, LANES), lambda bi, ci: (bi, ci, 0)),
            pl.BlockSpec((taps, 2 * mw), const2),
            pl.BlockSpec((1, 2 * mw), const2),
            pl.BlockSpec((1, LANES), const2),
            pl.BlockSpec((1, mw), const2),
            pl.BlockSpec((None, nb, heads, dh, dh), lambda bi, ci: (state_layer, bi, 0, 0, 0)),
            pl.BlockSpec((None, nb, heads, dh), st_in),
            pl.BlockSpec((None, nb, heads, LANES), st_in),
            pl.BlockSpec((None, nb, taps - 1, 2 * mw), st_in),
        ],
        inputs=[u, u, u, u, u_if, conv_w, conv_b.reshape(1, 2 * mw), bif_pad, norm_w.reshape(1, mw),
                c0, n0, m0, buf0],
        out_specs=[
            pl.BlockSpec((nb, chunk, mw), lambda bi, ci: (bi, ci, 0)),
            pl.BlockSpec((nslab, nb, heads, dh, dh), lambda bi, ci: (slab, bi, 0, 0, 0)),
            pl.BlockSpec((nslab, nb, heads, dh), st_out),
            pl.BlockSpec((nslab, nb, heads, LANES), st_out),
            pl.BlockSpec((nslab, nb, taps - 1, 2 * mw), st_out),
        ],
        out_shape=[
            jax.ShapeDtypeStruct((b, t, mw), BF16),
            jax.ShapeDtypeStruct((depth, b, heads, dh, dh), F32),
            jax.ShapeDtypeStruct((depth, b, heads, dh), F32),
            jax.ShapeDtypeStruct((depth, b, heads, LANES), F32),
            jax.ShapeDtypeStruct((depth, b, taps - 1, 2 * mw), F32),
        ],
        n_stacked=4, prev=prev,
        scratch_shapes=[pltpu.VMEM((nb, chunk + SUBLANES, 2 * mw), F32),
                        pltpu.VMEM((nb * heads, SUBLANES, LANES), F32)],
        compiler_params=_params(("parallel", "arbitrary")),
        name="mlstm_branch",
    )
    return res[0], tuple(res[1:])


def _wprep_kernel(w_hbm, o_ref, buf, sems, *, n_head_blk, skip):
    l, k = pl.program_id(0), pl.program_id(1)
    nk = pl.num_programs(1)
    tn = o_ref.shape[0]

    def copy(kk, slot):
        off = pl.multiple_of(kk * tn + jnp.where(kk >= n_head_blk, skip, 0), SUBLANES)
        return pltpu.make_async_copy(w_hbm.at[l, pl.ds(off, tn), :], buf.at[slot], sems.at[slot])

    slot = k % 2

    @pl.when(k == 0)
    def _():
        copy(k, slot).start()

    @pl.when(k + 1 < nk)
    def _():
        copy(k + 1, 1 - slot).start()

    copy(k, slot).wait()
    o_ref[...] = buf[slot].astype(o_ref.dtype)


def prepare_in_weight(w_t_all, *, head_rows, skip, tn):
    depth, p_in, d = w_t_all.shape
    n = p_in - skip
    assert head_rows % tn == 0 and n % tn == 0 and skip % SUBLANES == 0
    kern = functools.partial(_wprep_kernel, n_head_blk=head_rows // tn, skip=skip)
    return pl.pallas_call(
        kern,
        grid=(depth, n // tn),
        in_specs=[pl.BlockSpec(memory_space=pl.ANY)],
        out_specs=pl.BlockSpec((None, tn, d), lambda l, k: (l, k, 0)),
        out_shape=jax.ShapeDtypeStruct((depth, n, d), BF16),
        scratch_shapes=[pltpu.VMEM((2, tn, d), F32), pltpu.SemaphoreType.DMA((2,))],
        compiler_params=_params(("arbitrary", "arbitrary")),
        name="prepare_in_weight",
    )(w_t_all)


def _project_in(x, lw, *, layer, depth, prev, tm):
    b, t, d = x.shape
    col = lw["col"]
    return in_projection(
        x.reshape(b * t, d), lw["norm_pre"], lw["w_t"], lw["w_if"], layer=layer, depth=depth,
        heads=lw["heads"], k_col=col["ak"], v_col=col["av"], prev_kv=None if prev is None else prev[0],
        tm=tm, tn=lw["aw"])


def _finish_group(x, u2, uif2, o_a, lw, states, *, layer, depth, prev, chunk, nb, tm):
    b, t, d = x.shape
    aw, mw, col = lw["aw"], lw["mw"], lw["col"]
    c0, n0, m0, buf0, state_layer = states
    o_m, st = mlstm_branch(u2.reshape(b, t, -1), uif2.reshape(b, t, LANES), lw["conv_w"], lw["conv_b"], lw["b_if"],
                           lw["mlstm_norm_w"], c0, n0, m0, buf0, state_layer=state_layer, layer=layer,
                           depth=depth, prev=None if prev is None else prev[1], chunk=chunk, nb=nb,
                           qk_col=col["mq"], v_col=col["mv"], og_col=col["mo"], z_col=col["mz"])
    y = gated_merge(o_a.reshape(b * t, aw), o_m.reshape(b * t, mw), lw["w_pa"], lw["w_pm"], u2,
                    col["ga"], col["gm"], tm=min(tm, 512), tn=1024)
    x_new = out_projection(y, lw["w_out"], lw["norm_post"], x.reshape(b * t, d), tm=min(tm, 512)).reshape(b, t, d)
    return x_new, st


def kernel(x_prompt, x_sample, cache_k, cache_v, state_C, state_n, state_m, state_conv, page_table,
           norm_pre, norm_post, w_in, b_if, conv_w, conv_b, lambda_qk, attn_norm_w, mlstm_norm_w,
           w_pa, w_pm, w_out):
    depth = w_in.shape[0]
    d_model = x_prompt.shape[-1]
    heads = cache_k.shape[3]
    dqk = lambda_qk.shape[-1]
    dh = 2 * dqk
    aw = heads * dh
    m_heads, m_dh = state_C.shape[2], state_C.shape[3]
    mw = m_heads * m_dh
    bp, sp, _ = x_prompt.shape
    bs, ts, _ = x_sample.shape
    taps = conv_w.shape[1]
    n_if = 2 * m_heads
    if_col = 4 * aw + 4 * mw
    col = {"aq": 0, "ak": aw, "av": 2 * aw, "az": 3 * aw, "mq": 4 * aw, "mv": 4 * aw + 2 * mw,
           "mo": 4 * aw + 3 * mw, "mz": if_col, "ga": if_col + mw, "gm": if_col + mw + d_model}

    zero_states = (jnp.zeros((1, bp, m_heads, m_dh, m_dh), F32), jnp.zeros((1, bp, m_heads, m_dh), F32),
                   jnp.zeros((1, bp, m_heads, LANES), F32), jnp.zeros((1, bp, taps - 1, 2 * mw), F32))
    m_rep = jnp.broadcast_to(state_m[..., None], state_m.shape + (LANES,))

    w_t_all = prepare_in_weight(jnp.swapaxes(w_in, 1, 2), head_rows=if_col, skip=n_if, tn=aw)
    tq = 512
    tm_p, tm_s = min(1024, bp * sp), bs * ts
    fuse = bp * heads * (sp // tq) == bs

    xp, xs = x_prompt, x_sample
    prev_p = prev_s = None
    for l in range(depth):
        lam_init = 0.8 - 0.6 * math.exp(-0.3 * l)
        lq = lambda_qk[l].astype(F32)
        lam = jnp.exp(jnp.sum(lq[0] * lq[1])) - jnp.exp(jnp.sum(lq[2] * lq[3])) + lam_init
        scal = jnp.reshape(lam, (1,)).astype(F32)
        w_gates = jnp.swapaxes(w_in[l, :, if_col:if_col + n_if], 0, 1)
        lw = dict(heads=heads, aw=aw, mw=mw, col=col, norm_pre=norm_pre[l], norm_post=norm_post[l],
                  w_t=w_t_all, w_if=jnp.pad(w_gates, ((0, LANES - n_if), (0, 0))),
                  conv_w=conv_w[l], conv_b=conv_b[l], b_if=b_if[l], mlstm_norm_w=mlstm_norm_w[l],
                  w_pa=w_pa[l].astype(BF16), w_pm=w_pm[l].astype(BF16), w_out=w_out[l].astype(BF16))
        attn_cols = dict(dqk=dqk, q_col=col["aq"], z_col=col["az"], post_scale=1.0 - lam_init)

        u2p, uifp, kp_st, vp_st = _project_in(xp, lw, layer=l, depth=depth, prev=prev_p, tm=tm_p)
        u2s, uifs, ks_st, vs_st = _project_in(xs, lw, layer=l, depth=depth, prev=prev_s, tm=tm_s)
        u_p, u_s = u2p.reshape(bp, sp, -1), u2s.reshape(bs, ts, -1)
        if fuse:
            oa_p, oa_s = fused_attention(u_p, u_s, ks_st, vs_st, l, cache_k, cache_v, l, page_table, scal,
                                         attn_norm_w[l], k_col=col["ak"], v_col=col["av"], tq=tq, **attn_cols)
        else:
            oa_p = prompt_attention(u_p, scal, attn_norm_w[l], heads=heads, k_col=col["ak"], v_col=col["av"],
                                    tq=tq, **attn_cols)
            oa_s = sample_attention(u_s, ks_st, vs_st, l, cache_k, cache_v, l, page_table, scal, attn_norm_w[l],
                                    **attn_cols)
        xp, st_p = _finish_group(xp, u2p, uifp, oa_p, lw, zero_states + (0,), layer=l, depth=depth, prev=prev_p,
                                 chunk=CHUNK if sp % CHUNK == 0 else sp, nb=1, tm=tm_p)
        xs, st_s = _finish_group(xs, u2s, uifs, oa_s, lw, (state_C, state_n, m_rep, state_conv, l), layer=l,
                                 depth=depth, prev=prev_s, chunk=CHUNK if ts % CHUNK == 0 else ts, nb=4, tm=tm_s)
        prev_p, prev_s = ((kp_st, vp_st), st_p), ((ks_st, vs_st), st_s)

    def unpack(prev, b, t):
        (k_st, v_st), (c, n, m, tail) = prev
        return (k_st.reshape(depth, b, t, heads, dh), v_st.reshape(depth, b, t, heads, dh), c, n, m[..., 0], tail)

    return (xp, xs) + unpack(prev_p, bp, sp) + unpack(prev_s, bs, ts)
```

```python
import functools
import math

import jax
import jax.numpy as jnp
import numpy as np
from jax import lax
from jax.experimental import pallas as pl
from jax.experimental.pallas import tpu as pltpu

F32 = jnp.float32
BF16 = jnp.bfloat16

EPS = 1e-6
CHUNK = 64
LANES = 128
SUBLANES = 8
BF16_ROWS = 16
NEG_BIG = -1e30
VMEM_LIMIT = 56 * 1024 * 1024
VMEM_LIMIT_FUSED = 60 * 1024 * 1024
LOG2E = math.log2(math.e)
KV_GROUP = 2


def _sigmoid(x):
    return 1.0 / (1.0 + jnp.exp(-x))


def _silu(x):
    return x * _sigmoid(x)


def _params(sem, vmem_limit=VMEM_LIMIT):
    return pltpu.CompilerParams(dimension_semantics=sem, vmem_limit_bytes=vmem_limit)


def _slabs(prev, layer, depth):
    return (depth, 0) if prev is None else (1, layer)


def _stacked_call(kernel_fn, *, grid, in_specs, inputs, out_specs, out_shape, n_stacked, prev, **kw):
    aliases = {}
    if prev is not None:
        first = len(out_shape) - n_stacked
        for k, p in enumerate(prev):
            aliases[len(inputs) + k] = first + k
        in_specs = list(in_specs) + [pl.BlockSpec(memory_space=pl.ANY)] * len(prev)
        inputs = list(inputs) + list(prev)
    return pl.pallas_call(kernel_fn, grid=grid, in_specs=in_specs, out_specs=out_specs, out_shape=out_shape,
                          input_output_aliases=aliases, **kw)(*inputs)


def _dot_nt(a, b):
    return lax.dot_general(a, b, (((1,), (1,)), ((), ())), preferred_element_type=F32)


def _inproj_kernel(x_ref, nw_ref, w_ref, wif_ref, *rest, n_prev, k_blk, v_blk, heads, nslab, slab):
    u_ref, uif_ref, k_hbm, v_hbm, h_scr, k_stage, v_stage, sems = rest[n_prev:]
    i, j = pl.program_id(0), pl.program_id(1)
    tm = x_ref.shape[0]
    dh = k_stage.shape[-1]
    rows = tm * heads
    row0 = pl.multiple_of(i * rows, rows)
    targets = ((k_blk, k_stage, k_hbm), (v_blk, v_stage, v_hbm))

    def copies(which):
        _, stage, hbm = targets[which]
        return [pltpu.make_async_copy(stage, hbm.at[slab + s, pl.ds(row0, rows), :], sems.at[which, s])
                for s in range(nslab)]

    @pl.when(j == 0)
    def _():
        x = x_ref[...]
        ms = jnp.mean(x * x, axis=-1, keepdims=True)
        h = (x * lax.rsqrt(ms + EPS) * nw_ref[...]).astype(BF16)
        h_scr[...] = h
        uif_ref[...] = _dot_nt(h, wif_ref[...].astype(BF16))

    acc = _dot_nt(h_scr[...], w_ref[...])
    u_ref[...] = acc

    for which, (blk, stage, _) in enumerate(targets):
        @pl.when(j == blk)
        def _():
            for h in range(heads):
                stage[pl.ds(h, tm, stride=heads), :] = acc[:, h * dh:(h + 1) * dh]
            for c in copies(which):
                c.start()

    @pl.when(j == pl.num_programs(1) - 1)
    def _():
        for which in range(len(targets)):
            for c in copies(which):
                c.wait()


def in_projection(x, norm_w, w_t, w_if, *, layer, depth, heads, k_col, v_col, prev_kv, tm, tn):
    t, d = x.shape
    n = w_t.shape[1]
    dh = tn // heads
    k_blk, v_blk = k_col // tn, v_col // tn
    assert k_col % tn == 0 and v_col % tn == 0 and tn == heads * dh and max(k_blk, v_blk) < n // tn - 1
    nslab, slab = _slabs(prev_kv, layer, depth)
    kern = functools.partial(_inproj_kernel, n_prev=0 if prev_kv is None else 2, k_blk=k_blk, v_blk=v_blk,
                             heads=heads, nslab=nslab, slab=slab)
    kv_spec = pl.BlockSpec(memory_space=pl.ANY)
    kv_shape = jax.ShapeDtypeStruct((depth, t * heads, dh), F32)
    stage = pltpu.VMEM((tm * heads, dh), F32)
    return _stacked_call(
        kern,
        grid=(t // tm, n // tn),
        in_specs=[
            pl.BlockSpec((tm, d), lambda i, j: (i, 0)),
            pl.BlockSpec((1, d), lambda i, j: (0, 0)),
            pl.BlockSpec((None, tn, d), lambda i, j: (layer, j, 0)),
            pl.BlockSpec((LANES, d), lambda i, j: (0, 0)),
        ],
        inputs=[x, norm_w.reshape(1, d), w_t, w_if],
        out_specs=[
            pl.BlockSpec((tm, tn), lambda i, j: (i, j)),
            pl.BlockSpec((tm, LANES), lambda i, j: (i, 0)),
            kv_spec, kv_spec,
        ],
        out_shape=[jax.ShapeDtypeStruct((t, n), F32), jax.ShapeDtypeStruct((t, LANES), F32), kv_shape, kv_shape],
        n_stacked=2, prev=prev_kv,
        scratch_shapes=[pltpu.VMEM((tm, d), BF16), stage, stage, pltpu.SemaphoreType.DMA((2, nslab))],
        compiler_params=_params(("arbitrary", "arbitrary")),
        name="in_projection",
    )


def _merge_kernel(oa_ref, om_ref, *refs, n_col):
    y_ref = refs[4 * n_col]
    tn = refs[0].shape[1]
    for c in range(n_col):
        wpa_ref, wpm_ref, ga_ref, gm_ref = refs[4 * c:4 * c + 4]
        pa = jnp.dot(oa_ref[...], wpa_ref[...], preferred_element_type=F32)
        pm = jnp.dot(om_ref[...], wpm_ref[...], preferred_element_type=F32)
        y = _sigmoid(ga_ref[...]) * pa + _sigmoid(gm_ref[...]) * pm
        y_ref[:, c * tn:(c + 1) * tn] = y.astype(y_ref.dtype)


def gated_merge(o_a, o_m, w_pa, w_pm, u, ga_col, gm_col, *, tm, tn):
    t, wa = o_a.shape
    wm = o_m.shape[1]
    d = w_pa.shape[1]
    assert ga_col % tn == 0 and gm_col % tn == 0 and d % tn == 0
    n_col = d // tn
    specs, args = [], []
    for c in range(n_col):
        specs += [pl.BlockSpec((wa, tn), lambda i, c=c: (0, c)),
                  pl.BlockSpec((wm, tn), lambda i, c=c: (0, c)),
                  pl.BlockSpec((tm, tn), lambda i, c=c: (i, ga_col // tn + c)),
                  pl.BlockSpec((tm, tn), lambda i, c=c: (i, gm_col // tn + c))]
        args += [w_pa, w_pm, u, u]
    return pl.pallas_call(
        functools.partial(_merge_kernel, n_col=n_col),
        grid=(t // tm,),
        in_specs=[pl.BlockSpec((tm, wa), lambda i: (i, 0)), pl.BlockSpec((tm, wm), lambda i: (i, 0))] + specs,
        out_specs=pl.BlockSpec((tm, d), lambda i: (i, 0)),
        out_shape=jax.ShapeDtypeStruct((t, d), BF16),
        compiler_params=_params(("parallel",)),
        name="gated_merge",
    )(o_a, o_m, *args)


def _outproj_kernel(y_ref, w_ref, nw_ref, x_ref, o_ref):
    out = jnp.dot(y_ref[...], w_ref[...], preferred_element_type=F32)
    ms = jnp.mean(out * out, axis=-1, keepdims=True)
    o_ref[...] = x_ref[...] + out * lax.rsqrt(ms + EPS) * nw_ref[...]


def out_projection(y, w_out, norm_w, x, *, tm):
    t, d = x.shape
    return pl.pallas_call(
        _outproj_kernel,
        grid=(t // tm,),
        in_specs=[
            pl.BlockSpec((tm, d), lambda i: (i, 0)),
            pl.BlockSpec((d, d), lambda i: (0, 0)),
            pl.BlockSpec((1, d), lambda i: (0, 0)),
            pl.BlockSpec((tm, d), lambda i: (i, 0)),
        ],
        out_specs=pl.BlockSpec((tm, d), lambda i: (i, 0)),
        out_shape=jax.ShapeDtypeStruct((t, d), F32),
        compiler_params=_params(("parallel",)),
        name="out_projection",
    )(y, w_out, norm_w.reshape(1, d), x)


def _prompt_attn_kernel(*refs, **kw):
    _prompt_attn_body(pl.program_id(2), *refs, **kw)


def _prompt_attn_body(qi, scal_ref, q_ref, k_ref, v_ref, z_ref, nw_ref, o_ref, kb_scr, vt_scr,
                      *, tq, dqk, scale_log2, post_scale):
    lam = scal_ref[0]
    dv = v_ref.shape[-1]
    n_blk = vt_scr.shape[0]

    @pl.when(qi == 0)
    def _():
        kb_scr[...] = k_ref[0].astype(BF16)
        for c in range(n_blk):
            vt_scr[c, 0:dv, :] = jnp.transpose(v_ref[0, c * tq:(c + 1) * tq, :]).astype(BF16)
            vt_scr[c, dv:, :] = jnp.ones((BF16_ROWS, tq), BF16)

    q = q_ref[0] * scale_log2
    lane = lax.broadcasted_iota(jnp.int32, q.shape, 1)
    q1 = jnp.where(lane < dqk, q, 0.0).astype(BF16)
    q2 = jnp.where(lane >= dqk, q, 0.0).astype(BF16)

    def scores_t(k, qm):
        return lax.dot_general(k, qm, (((1,), (1,)), ((), ())), preferred_element_type=F32)

    def online(state, s_t, vt):
        m, acc = state
        m_new = jnp.maximum(m, jnp.max(s_t, axis=0, keepdims=True))
        alpha = jnp.exp2(m - m_new)
        p = jnp.exp2(s_t - m_new)
        acc = alpha * acc + jnp.dot(vt, p.astype(BF16), preferred_element_type=F32)
        return m_new, acc

    def scores_of(j):
        k = kb_scr[pl.ds(pl.multiple_of(j * tq, tq), tq), :]
        return scores_t(k, q1), scores_t(k, q2)

    def causal(s):
        krow = lax.broadcasted_iota(jnp.int32, s.shape, 0)
        qcol = lax.broadcasted_iota(jnp.int32, s.shape, 1)
        return jnp.where(krow <= qcol, s, -jnp.inf)

    def group(j, carry, n, last_is_diagonal):
        st1, st2 = carry
        scores = [scores_of(j + i) for i in range(n)]
        for i, (s1, s2) in enumerate(scores):
            if last_is_diagonal and i == n - 1:
                s1, s2 = causal(s1), causal(s2)
            vt = vt_scr[j + i]
            st1, st2 = online(st1, s1, vt), online(st2, s2, vt)
        return st1, st2

    def init():
        return jnp.full((1, tq), NEG_BIG, F32), jnp.zeros((dv + BF16_ROWS, tq), F32)

    n_full = qi // KV_GROUP
    carry = lax.fori_loop(0, n_full, lambda g, c: group(KV_GROUP * g, c, KV_GROUP, False), (init(), init()))
    tails = [functools.partial(group, n=r + 1, last_is_diagonal=True) for r in range(KV_GROUP)]
    rest0 = n_full * KV_GROUP
    (_, a1), (_, a2) = lax.switch(qi - rest0, [lambda c, f=f: f(rest0, c) for f in tails], carry)

    out_t = a1[0:dv] / a1[dv:dv + 1] - lam * (a2[0:dv] / a2[dv:dv + 1])
    out = jnp.transpose(out_t)
    ms = jnp.mean(out * out, axis=-1, keepdims=True)
    y = out * lax.rsqrt(ms + EPS) * nw_ref[...] * post_scale
    o_ref[0] = (y * _silu(z_ref[0])).astype(o_ref.dtype)


def prompt_attention(u, scal, norm_w, *, heads, dqk, q_col, k_col, v_col, z_col, post_scale, tq):
    b, s, _ = u.shape
    dh = 2 * dqk
    qb, kb, vb, zb = q_col // dh, k_col // dh, v_col // dh, z_col // dh
    kern = functools.partial(_prompt_attn_kernel, tq=tq, dqk=dqk, scale_log2=dqk ** -0.5 * LOG2E,
                             post_scale=post_scale)
    return pl.pallas_call(
        kern,
        grid=(b, heads, s // tq),
        in_specs=[
            pl.BlockSpec(memory_space=pltpu.SMEM),
            pl.BlockSpec((1, tq, dh), lambda bi, h, i: (bi, i, qb + h)),
            pl.BlockSpec((1, s, dh), lambda bi, h, i: (bi, 0, kb + h)),
            pl.BlockSpec((1, s, dh), lambda bi, h, i: (bi, 0, vb + h)),
            pl.BlockSpec((1, tq, dh), lambda bi, h, i: (bi, i, zb + h)),
            pl.BlockSpec((1, dh), lambda bi, h, i: (0, h)),
        ],
        out_specs=pl.BlockSpec((1, tq, dh), lambda bi, h, i: (bi, i, h)),
        out_shape=jax.ShapeDtypeStruct((b, s, heads * dh), BF16),
        scratch_shapes=[pltpu.VMEM((s, dh), BF16), pltpu.VMEM((s // tq, dh + BF16_ROWS, tq), BF16)],
        compiler_params=_params(("parallel", "parallel", "arbitrary")),
        name="prompt_attention",
    )(scal, u, u, u, u, norm_w.reshape(1, heads * dh))


def _sample_attn_kernel(pt_ref, scal_ref, q_ref, kn_ref, vn_ref, z_ref, nw_ref, bias_ref, biasn_ref, *rest,
                        n_pages, **kw):
    del pt_ref
    _sample_attn_body(scal_ref, q_ref, kn_ref, vn_ref, z_ref, nw_ref, bias_ref, biasn_ref,
                      rest[:n_pages], rest[n_pages:2 * n_pages], rest[2 * n_pages], rest[2 * n_pages + 1], **kw)


def _sample_attn_body(scal_ref, q_ref, kn_ref, vn_ref, z_ref, nw_ref, bias_ref, biasn_ref, k_refs, v_refs,
                      o_ref, s_scr, *, heads, dqk, nq, scale, post_scale):
    n_pages = len(k_refs)
    lam = scal_ref[0]
    dh = 2 * dqk
    rows_pp = k_refs[0].shape[0] * k_refs[0].shape[1]

    q = q_ref[0] * scale
    lane = lax.broadcasted_iota(jnp.int32, (nq, dh), 1)
    pieces = []
    for h in range(heads):
        qh = q[:, h * dh:(h + 1) * dh]
        pieces.append(jnp.where(lane < dqk, qh, 0.0))
        pieces.append(jnp.where(lane >= dqk, qh, 0.0))
    qall = jnp.concatenate(pieces, axis=0).astype(BF16)
    nrow = qall.shape[0]

    def scores(k2d):
        return lax.dot_general(qall, k2d, (((1,), (1,)), ((), ())), preferred_element_type=F32)

    bias = bias_ref[...]
    slabs = [slice(c * LANES, (c + 1) * LANES) for c in range(rows_pp // LANES)]
    m_acc = jnp.full((nrow, LANES), NEG_BIG, F32)
    for j in range(n_pages):
        kj = k_refs[j][...].reshape(rows_pp, dh).astype(BF16)
        s = scores(kj)
        for c in slabs:
            sc = s[:, c] + bias
            s_scr[:, j * rows_pp + c.start:j * rows_pp + c.stop] = sc
            m_acc = jnp.maximum(m_acc, sc)
    sn = scores(kn_ref[...].astype(BF16)) + biasn_ref[...]
    m = jnp.maximum(jnp.max(m_acc, axis=-1, keepdims=True), jnp.max(sn, axis=-1, keepdims=True))

    l_acc = jnp.zeros((nrow, LANES), F32)
    acc = jnp.zeros((nrow, dh), F32)
    for j in range(n_pages):
        p = jnp.exp(s_scr[:, j * rows_pp:(j + 1) * rows_pp] - m)
        for c in slabs:
            l_acc = l_acc + p[:, c]
        vj = v_refs[j][...].reshape(rows_pp, dh).astype(BF16)
        acc = acc + jnp.dot(p.astype(BF16), vj, preferred_element_type=F32)
    pn = jnp.exp(sn - m)
    l = jnp.sum(l_acc, axis=-1, keepdims=True) + jnp.sum(pn, axis=-1, keepdims=True)
    acc = acc + jnp.dot(pn.astype(BF16), vn_ref[...].astype(BF16), preferred_element_type=F32)

    o = acc / l
    z = z_ref[0]
    nw = nw_ref[...]
    for h in range(heads):
        r0 = h * 2 * nq
        oh = o[r0:r0 + nq] - lam * o[r0 + nq:r0 + 2 * nq]
        ms = jnp.mean(oh * oh, axis=-1, keepdims=True)
        y = oh * lax.rsqrt(ms + EPS) * nw[:, h * dh:(h + 1) * dh] * post_scale
        o_ref[0, :, h * dh:(h + 1) * dh] = (y * _silu(z[:, h * dh:(h + 1) * dh])).astype(o_ref.dtype)


def _sample_masks(heads, nq, page_rows):
    r = np.arange(heads * 2 * nq)
    rh, rt = r // (2 * nq), r % nq
    c = np.arange(page_rows)
    past = np.where((c[None, :] % heads) == rh[:, None], 0.0, -np.inf).astype(np.float32)
    cn = np.arange(nq * heads)
    ok = ((cn[None, :] % heads) == rh[:, None]) & ((cn[None, :] // heads) <= rt[:, None])
    new = np.where(ok, 0.0, -np.inf).astype(np.float32)
    return jnp.asarray(past), jnp.asarray(new)


def sample_attention(u, k_rows, v_rows, kv_layer, cache_k, cache_v, layer, page_table, scal, norm_w,
                     *, dqk, q_col, z_col, post_scale):
    b, nq, _ = u.shape
    _, _, page, heads, dh = cache_k.shape
    n_pages = page_table.shape[1]
    width = heads * dh
    qb, zb = q_col // width, z_col // width
    assert LANES % heads == 0 and (page * heads) % LANES == 0
    bias, bias_new = _sample_masks(heads, nq, LANES)
    nrow = heads * 2 * nq

    def page_spec(j):
        return pl.BlockSpec((None, None, page, heads, dh), lambda i, pt: (layer, pt[i, j], 0, 0, 0))

    new_spec = pl.BlockSpec((None, nq * heads, dh), lambda i, pt: (kv_layer, i, 0))
    kern = functools.partial(_sample_attn_kernel, n_pages=n_pages, heads=heads, dqk=dqk, nq=nq,
                             scale=dqk ** -0.5, post_scale=post_scale)
    grid_spec = pltpu.PrefetchScalarGridSpec(
        num_scalar_prefetch=1,
        grid=(b,),
        in_specs=[
            pl.BlockSpec(memory_space=pltpu.SMEM),
            pl.BlockSpec((1, nq, width), lambda i, pt: (i, 0, qb)),
            new_spec, new_spec,
            pl.BlockSpec((1, nq, width), lambda i, pt: (i, 0, zb)),
            pl.BlockSpec((1, width), lambda i, pt: (0, 0)),
            pl.BlockSpec(bias.shape, lambda i, pt: (0, 0)),
            pl.BlockSpec(bias_new.shape, lambda i, pt: (0, 0)),
        ] + [page_spec(j) for j in range(n_pages)] * 2,
        out_specs=pl.BlockSpec((1, nq, width), lambda i, pt: (i, 0, 0)),
        scratch_shapes=[pltpu.VMEM((nrow, n_pages * page * heads), F32)],
    )
    return pl.pallas_call(
        kern,
        grid_spec=grid_spec,
        out_shape=jax.ShapeDtypeStruct((b, nq, width), BF16),
        compiler_params=_params(("arbitrary",)),
        name="sample_attention",
    )(page_table, scal, u, k_rows, v_rows, u, norm_w.reshape(1, width), bias, bias_new,
      *([cache_k] * n_pages), *([cache_v] * n_pages))


def _fused_attn_kernel(pt_ref, scal_ref, qp_ref, kp_ref, vp_ref, zp_ref, nwp_ref,
                       qs_ref, kn_ref, vn_ref, zs_ref, nws_ref, bias_ref, biasn_ref, *rest,
                       n_pages, n_qblk, heads, dqk, nq, tq, post_scale, n_prev, m_heads, taps):
    del pt_ref
    k_refs, v_refs = rest[:n_pages], rest[n_pages:2 * n_pages]
    rest = rest[2 * n_pages:]
    mlstm_in, rest = rest[:13], rest[13 + n_prev:]
    op_ref, os_ref = rest[:2]
    mlstm_out = rest[2:7]
    kb_scr, vt_scr, s_scr, ext_scr, m_scr = rest[7:]
    _sample_attn_body(scal_ref, qs_ref, kn_ref, vn_ref, zs_ref, nws_ref, bias_ref, biasn_ref, k_refs, v_refs,
                      os_ref, s_scr, heads=heads, dqk=dqk, nq=nq, scale=dqk ** -0.5, post_scale=post_scale)
    _mlstm_body(0, 1, *mlstm_in, *mlstm_out, ext_scr, m_scr, heads=m_heads, taps=taps, single_chunk=True)
    _prompt_attn_body(pl.program_id(0) % n_qblk, scal_ref, qp_ref, kp_ref, vp_ref, zp_ref, nwp_ref, op_ref,
                      kb_scr, vt_scr, tq=tq, dqk=dqk, scale_log2=dqk ** -0.5 * LOG2E, post_scale=post_scale)


def fused_attention(u_p, u_s, k_rows, v_rows, kv_layer, cache_k, cache_v, layer, page_table, scal, norm_w,
                    mlstm_args, mlstm_kw, *, dqk, q_col, k_col, v_col, z_col, post_scale, tq):
    bp, sp, _ = u_p.shape
    bs, nq, _ = u_s.shape
    _, _, page, heads, dh = cache_k.shape
    n_pages = page_table.shape[1]
    width = heads * dh
    n_qblk = sp // tq
    assert bp * heads * n_qblk == bs
    qb, kb, vb, zb = q_col // dh, k_col // dh, v_col // dh, z_col // dh
    qbs, zbs = q_col // width, z_col // width
    assert LANES % heads == 0 and (page * heads) % LANES == 0
    bias, bias_new = _sample_masks(heads, nq, LANES)

    def pb(s):
        return s // (heads * n_qblk)

    def ph(s):
        return (s // n_qblk) % heads

    def pq(s):
        return s % n_qblk

    def page_spec(j):
        return pl.BlockSpec((None, None, page, heads, dh), lambda s, pt: (layer, pt[s, j], 0, 0, 0))

    new_spec = pl.BlockSpec((None, nq * heads, dh), lambda s, pt: (kv_layer, s, 0))
    prev = mlstm_kw["prev"]
    mio = _mlstm_io(*mlstm_args, **mlstm_kw, chunk=nq, nb=1, wrap=lambda f: (lambda s, pt: f(s, 0)))
    n_prev = 0 if prev is None else len(prev)
    kern = functools.partial(_fused_attn_kernel, n_pages=n_pages, n_qblk=n_qblk, heads=heads, dqk=dqk, nq=nq,
                             tq=tq, post_scale=post_scale, n_prev=n_prev, m_heads=mlstm_args[6].shape[2],
                             taps=mlstm_args[2].shape[0])
    nw2 = norm_w.reshape(1, width)
    in_specs = [
        pl.BlockSpec(memory_space=pltpu.SMEM),
        pl.BlockSpec((1, tq, dh), lambda s, pt: (pb(s), pq(s), qb + ph(s))),
        pl.BlockSpec((1, sp, dh), lambda s, pt: (pb(s), 0, kb + ph(s))),
        pl.BlockSpec((1, sp, dh), lambda s, pt: (pb(s), 0, vb + ph(s))),
        pl.BlockSpec((1, tq, dh), lambda s, pt: (pb(s), pq(s), zb + ph(s))),
        pl.BlockSpec((1, dh), lambda s, pt: (0, ph(s))),
        pl.BlockSpec((1, nq, width), lambda s, pt: (s, 0, qbs)),
        new_spec, new_spec,
        pl.BlockSpec((1, nq, width), lambda s, pt: (s, 0, zbs)),
        pl.BlockSpec((1, width), lambda s, pt: (0, 0)),
        pl.BlockSpec(bias.shape, lambda s, pt: (0, 0)),
        pl.BlockSpec(bias_new.shape, lambda s, pt: (0, 0)),
    ] + [page_spec(j) for j in range(n_pages)] * 2 + mio["in_specs"]
    inputs = [scal, u_p, u_p, u_p, u_p, nw2, u_s, k_rows, v_rows, u_s, nw2, bias, bias_new,
              *([cache_k] * n_pages), *([cache_v] * n_pages)] + mio["inputs"]
    aliases = {}
    if prev is not None:
        for k, p in enumerate(prev):
            aliases[1 + len(inputs) + k] = 3 + k
        in_specs = in_specs + [pl.BlockSpec(memory_space=pl.ANY)] * n_prev
        inputs = inputs + list(prev)
    grid_spec = pltpu.PrefetchScalarGridSpec(
        num_scalar_prefetch=1,
        grid=(bs,),
        in_specs=in_specs,
        out_specs=[
            pl.BlockSpec((1, tq, dh), lambda s, pt: (pb(s), pq(s), ph(s))),
            pl.BlockSpec((1, nq, width), lambda s, pt: (s, 0, 0)),
        ] + mio["out_specs"],
        scratch_shapes=[pltpu.VMEM((sp, dh), BF16), pltpu.VMEM((n_qblk, dh + BF16_ROWS, tq), BF16),
                        pltpu.VMEM((heads * 2 * nq, n_pages * page * heads), F32)] + mio["scratch"],
    )
    res = pl.pallas_call(
        kern,
        grid_spec=grid_spec,
        out_shape=[jax.ShapeDtypeStruct((bp, sp, width), BF16), jax.ShapeDtypeStruct((bs, nq, width), BF16)]
        + mio["out_shape"],
        input_output_aliases=aliases,
        compiler_params=_params(("arbitrary",), VMEM_LIMIT_FUSED),
        name="fused_attention",
    )(page_table, *inputs)
    return res[0], res[1], res[2], tuple(res[3:])


def _mlstm_kernel(*refs, n_prev, **kw):
    _mlstm_body(pl.program_id(1), pl.num_programs(1), *refs[:13], *refs[13 + n_prev:], **kw)


def _mlstm_body(ci, nc, qk_ref, v_ref, og_ref, z_ref, if_ref, cw_ref, cb_ref, bif_ref, nw_ref,
                c0_ref, n0_ref, m0_ref, buf0_ref, om_ref, c_ref, n_ref, m_ref, tail_ref, ext_scr, m_scr,
                *, heads, taps, single_chunk):
    nb, L = qk_ref.shape[0], qk_ref.shape[1]
    mw = v_ref.shape[2]
    dh = mw // heads
    pad = SUBLANES
    nslab = c_ref.shape[0]

    if single_chunk:
        for bb in range(nb):
            ext_scr[bb, pad - (taps - 1):pad, :] = buf0_ref[bb]
    else:
        @pl.when(ci == 0)
        def _():
            c_ref[0] = c0_ref[...]
            n_ref[0] = n0_ref[...]
            for bb in range(nb):
                for h in range(heads):
                    m_scr[bb * heads + h] = jnp.broadcast_to(m0_ref[bb, h:h + 1, :], m_scr.shape[1:])
                ext_scr[bb, pad - (taps - 1):pad, :] = buf0_ref[bb]

    row = lax.broadcasted_iota(jnp.int32, (L, L), 0)
    col = lax.broadcasted_iota(jnp.int32, (L, L), 1)
    eye = row == col
    low = col <= row

    for bb in range(nb):
        ext_scr[bb, pad:pad + L, :] = qk_ref[bb]
        conv = cb_ref[...]
        for j in range(taps):
            r0 = pad - (taps - 1) + j
            conv = conv + ext_scr[bb, r0:r0 + L, :] * cw_ref[j:j + 1, :]

        ext_scr[bb, 0:pad, :] = ext_scr[bb, L:L + pad, :]
        qk = _silu(conv)
        q_all = qk[:, :mw]
        k_all = qk[:, mw:] * (dh ** -0.5)
        v_all = v_ref[bb]
        g = if_ref[bb] + bif_ref[...]
        lf_all = jnp.minimum(g, 0.0) - jnp.log(1.0 + jnp.exp(-jnp.abs(g)))
        for h in range(heads):
            li_col = g[:, h:h + 1]
            lf_col = lf_all[:, heads + h:heads + h + 1]
            lf_row = jnp.sum(jnp.where(eye, lf_col, 0.0), axis=0, keepdims=True)
            bt_col = jnp.sum(jnp.where(low, lf_row, 0.0), axis=1, keepdims=True)
            r_row = jnp.sum(jnp.where(eye, li_col - bt_col, 0.0), axis=0, keepdims=True)
            m_prev = m0_ref[bb, h:h + 1, 0:1] if single_chunk else m_scr[bb * heads + h, 0:1, 0:1]

            log_d = jnp.where(low, bt_col + r_row, -jnp.inf)
            log_inter = bt_col + m_prev
            m_t = jnp.maximum(log_inter, jnp.max(log_d, axis=1, keepdims=True))
            d_mat = jnp.exp(log_d - m_t)
            inter = jnp.exp(log_inter - m_t)

            sl = slice(h * dh, (h + 1) * dh)
            q_h, k_h = q_all[:, sl], k_all[:, sl]
            q_b, k_b, v_b = q_h.astype(BF16), k_h.astype(BF16), v_all[:, sl].astype(BF16)
            c_h = c0_ref[bb, h] if single_chunk else c_ref[0, bb, h]
            n_h = n0_ref[bb, h:h + 1, :] if single_chunk else n_ref[0, bb, h:h + 1, :]

            s = lax.dot_general(q_b, k_b, (((1,), (1,)), ((), ())), preferred_element_type=F32) * d_mat
            num = (jnp.dot(s.astype(BF16), v_b, preferred_element_type=F32)
                   + jnp.dot(q_b, c_h.astype(BF16), preferred_element_type=F32) * inter)
            den = jnp.sum(s, axis=1, keepdims=True) + inter * jnp.sum(q_h * n_h, axis=1, keepdims=True)
            den = jnp.maximum(jnp.abs(den), jnp.exp(-m_t))
            hh = num / den

            b_last = jnp.sum(lf_col, axis=0, keepdims=True)
            log_w = b_last - bt_col + li_col
            m_new = jnp.maximum(b_last + m_prev, jnp.max(log_w, axis=0, keepdims=True))
            wk = jnp.exp(log_w - m_new)
            decay = jnp.exp(b_last + m_prev - m_new)
            kw = k_h * wk
            upd = lax.dot_general(kw.astype(BF16), v_b, (((0,), (0,)), ((), ())), preferred_element_type=F32)
            c_new = decay * c_h + upd
            n_new = decay * n_h + jnp.sum(kw, axis=0, keepdims=True)
            for s in range(nslab if single_chunk else 1):
                c_ref[s, bb, h] = c_new
                n_ref[s, bb, h:h + 1, :] = n_new
            m_scr[bb * heads + h] = jnp.broadcast_to(m_new, m_scr.shape[1:])

            mu = jnp.mean(hh, axis=1, keepdims=True)
            xc = hh - mu
            var = jnp.mean(xc * xc, axis=1, keepdims=True)
            y = xc * lax.rsqrt(var + EPS) * nw_ref[:, sl] * _sigmoid(og_ref[bb, :, sl])
            om_ref[bb, :, sl] = (y * _silu(z_ref[bb, :, sl])).astype(om_ref.dtype)

    def finish():
        for s in range(nslab):
            if s > 0 and not single_chunk:
                c_ref[s] = c_ref[0]
                n_ref[s] = n_ref[0]
            for bb in range(nb):
                tail_ref[s, bb] = ext_scr[bb, pad - (taps - 1):pad, :]
                for h in range(heads):
                    m_ref[s, bb, h:h + 1, :] = m_scr[bb * heads + h, 0:1, :]

    if single_chunk:
        finish()
    else:
        pl.when(ci == nc - 1)(finish)


def mlstm_branch(u, u_if, conv_w, conv_b, b_if, norm_w, c0, n0, m0, buf0, *, state_layer, layer, depth, prev,
                 chunk, nb, qk_col, v_col, og_col, z_col):
    b, t, _ = u.shape
    heads = c0.shape[2]
    taps = conv_w.shape[0]
    io = _mlstm_io(u, u_if, conv_w, conv_b, b_if, norm_w, c0, n0, m0, buf0, state_layer=state_layer, layer=layer,
                   depth=depth, prev=prev, chunk=chunk, nb=nb, qk_col=qk_col, v_col=v_col, og_col=og_col,
                   z_col=z_col, wrap=lambda f: f)
    kern = functools.partial(_mlstm_kernel, n_prev=0 if prev is None else 4, heads=heads, taps=taps,
                             single_chunk=(t == chunk))
    res = _stacked_call(
        kern, grid=(b // nb, t // chunk), in_specs=io["in_specs"], inputs=io["inputs"], out_specs=io["out_specs"],
        out_shape=io["out_shape"], n_stacked=4, prev=prev, scratch_shapes=io["scratch"],
        compiler_params=_params(("parallel", "arbitrary")), name="mlstm_branch")
    return res[0], tuple(res[1:])


def _mlstm_io(u, u_if, conv_w, conv_b, b_if, norm_w, c0, n0, m0, buf0, *, state_layer, layer, depth, prev,
              chunk, nb, qk_col, v_col, og_col, z_col, wrap):
    b, t, _ = u.shape
    _, _, heads, dh, _ = c0.shape
    mw = heads * dh
    taps = conv_w.shape[0]
    assert chunk >= taps - 1 and t % chunk == 0 and b % nb == 0
    bif_pad = jnp.zeros((1, LANES), F32).at[0, :2 * heads].set(b_if)
    nslab, slab = _slabs(prev, layer, depth)

    def spec(shape, f):
        return pl.BlockSpec(shape, wrap(f))

    const2 = lambda bi, ci: (0, 0)
    st_in = lambda bi, ci: (state_layer, bi, 0, 0)
    st_out = lambda bi, ci: (slab, bi, 0, 0)
    return dict(
        in_specs=[
            spec((nb, chunk, 2 * mw), lambda bi, ci: (bi, ci, qk_col // (2 * mw))),
            spec((nb, chunk, mw), lambda bi, ci: (bi, ci, v_col // mw)),
            spec((nb, chunk, mw), lambda bi, ci: (bi, ci, og_col // mw)),
            spec((nb, chunk, mw), lambda bi, ci: (bi, ci, z_col // mw)),
            spec((nb, chunk, LANES), lambda bi, ci: (bi, ci, 0)),
            spec((taps, 2 * mw), const2),
            spec((1, 2 * mw), const2),
            spec((1, LANES), const2),
            spec((1, mw), const2),
            spec((None, nb, heads, dh, dh), lambda bi, ci: (state_layer, bi, 0, 0, 0)),
            spec((None, nb, heads, dh), st_in),
            spec((None, nb, heads, LANES), st_in),
            spec((None, nb, taps - 1, 2 * mw), st_in),
        ],
        inputs=[u, u, u, u, u_if, conv_w, conv_b.reshape(1, 2 * mw), bif_pad, norm_w.reshape(1, mw),
                c0, n0, m0, buf0],
        out_specs=[
            spec((nb, chunk, mw), lambda bi, ci: (bi, ci, 0)),
            spec((nslab, nb, heads, dh, dh), lambda bi, ci: (slab, bi, 0, 0, 0)),
            spec((nslab, nb, heads, dh), st_out),
            spec((nslab, nb, heads, LANES), st_out),
            spec((nslab, nb, taps - 1, 2 * mw), st_out),
        ],
        out_shape=[
            jax.ShapeDtypeStruct((b, t, mw), BF16),
            jax.ShapeDtypeStruct((depth, b, heads, dh, dh), F32),
            jax.ShapeDtypeStruct((depth, b, heads, dh), F32),
            jax.ShapeDtypeStruct((depth, b, heads, LANES), F32),
            jax.ShapeDtypeStruct((depth, b, taps - 1, 2 * mw), F32),
        ],
        scratch=[pltpu.VMEM((nb, chunk + SUBLANES, 2 * mw), F32),
                 pltpu.VMEM((nb * heads, SUBLANES, LANES), F32)],
    )


def _wprep_kernel(w_hbm, o_ref, buf, sems, *, n_head_blk, skip):
    l, k = pl.program_id(0), pl.program_id(1)
    nk = pl.num_programs(1)
    tn = o_ref.shape[0]

    def copy(kk, slot):
        off = pl.multiple_of(kk * tn + jnp.where(kk >= n_head_blk, skip, 0), SUBLANES)
        return pltpu.make_async_copy(w_hbm.at[l, pl.ds(off, tn), :], buf.at[slot], sems.at[slot])

    slot = k % 2

    @pl.when(k == 0)
    def _():
        copy(k, slot).start()

    @pl.when(k + 1 < nk)
    def _():
        copy(k + 1, 1 - slot).start()

    copy(k, slot).wait()
    o_ref[...] = buf[slot].astype(o_ref.dtype)


def prepare_in_weight(w_t_all, *, head_rows, skip, tn):
    depth, p_in, d = w_t_all.shape
    n = p_in - skip
    assert head_rows % tn == 0 and n % tn == 0 and skip % SUBLANES == 0
    kern = functools.partial(_wprep_kernel, n_head_blk=head_rows // tn, skip=skip)
    return pl.pallas_call(
        kern,
        grid=(depth, n // tn),
        in_specs=[pl.BlockSpec(memory_space=pl.ANY)],
        out_specs=pl.BlockSpec((None, tn, d), lambda l, k: (l, k, 0)),
        out_shape=jax.ShapeDtypeStruct((depth, n, d), BF16),
        scratch_shapes=[pltpu.VMEM((2, tn, d), F32), pltpu.SemaphoreType.DMA((2,))],
        compiler_params=_params(("arbitrary", "arbitrary")),
        name="prepare_in_weight",
    )(w_t_all)


def _project_in(x, lw, *, layer, depth, prev, tm):
    b, t, d = x.shape
    col = lw["col"]
    return in_projection(
        x.reshape(b * t, d), lw["norm_pre"], lw["w_t"], lw["w_if"], layer=layer, depth=depth,
        heads=lw["heads"], k_col=col["ak"], v_col=col["av"], prev_kv=None if prev is None else prev[0],
        tm=tm, tn=lw["aw"])


def _mlstm_operands(x, u2, uif2, lw, states, *, layer, depth, prev):
    b, t, _ = x.shape
    col = lw["col"]
    c0, n0, m0, buf0, state_layer = states
    args = (u2.reshape(b, t, -1), uif2.reshape(b, t, LANES), lw["conv_w"], lw["conv_b"], lw["b_if"],
            lw["mlstm_norm_w"], c0, n0, m0, buf0)
    kw = dict(state_layer=state_layer, layer=layer, depth=depth, prev=None if prev is None else prev[1],
              qk_col=col["mq"], v_col=col["mv"], og_col=col["mo"], z_col=col["mz"])
    return args, kw


def _finish_group(x, u2, uif2, o_a, lw, states, *, layer, depth, prev, chunk, nb, tm, mlstm_done=None):
    b, t, d = x.shape
    aw, mw, col = lw["aw"], lw["mw"], lw["col"]
    if mlstm_done is None:
        args, kw = _mlstm_operands(x, u2, uif2, lw, states, layer=layer, depth=depth, prev=prev)
        o_m, st = mlstm_branch(*args, **kw, chunk=chunk, nb=nb)
    else:
        o_m, st = mlstm_done
    y = gated_merge(o_a.reshape(b * t, aw), o_m.reshape(b * t, mw), lw["w_pa"], lw["w_pm"], u2,
                    col["ga"], col["gm"], tm=min(tm, 512), tn=1024)
    x_new = out_projection(y, lw["w_out"], lw["norm_post"], x.reshape(b * t, d), tm=min(tm, 512)).reshape(b, t, d)
    return x_new, st


def kernel(x_prompt, x_sample, cache_k, cache_v, state_C, state_n, state_m, state_conv, page_table,
           norm_pre, norm_post, w_in, b_if, conv_w, conv_b, lambda_qk, attn_norm_w, mlstm_norm_w,
           w_pa, w_pm, w_out):
    depth = w_in.shape[0]
    d_model = x_prompt.shape[-1]
    heads = cache_k.shape[3]
    dqk = lambda_qk.shape[-1]
    dh = 2 * dqk
    aw = heads * dh
    m_heads, m_dh = state_C.shape[2], state_C.shape[3]
    mw = m_heads * m_dh
    bp, sp, _ = x_prompt.shape
    bs, ts, _ = x_sample.shape
    taps = conv_w.shape[1]
    n_if = 2 * m_heads
    if_col = 4 * aw + 4 * mw
    col = {"aq": 0, "ak": aw, "av": 2 * aw, "az": 3 * aw, "mq": 4 * aw, "mv": 4 * aw + 2 * mw,
           "mo": 4 * aw + 3 * mw, "mz": if_col, "ga": if_col + mw, "gm": if_col + mw + d_model}

    zero_states = (jnp.zeros((1, bp, m_heads, m_dh, m_dh), F32), jnp.zeros((1, bp, m_heads, m_dh), F32),
                   jnp.zeros((1, bp, m_heads, LANES), F32), jnp.zeros((1, bp, taps - 1, 2 * mw), F32))
    m_rep = jnp.broadcast_to(state_m[..., None], state_m.shape + (LANES,))

    w_t_all = prepare_in_weight(jnp.swapaxes(w_in, 1, 2), head_rows=if_col, skip=n_if, tn=aw)
    tq = 512
    tm_p, tm_s = min(1024, bp * sp), bs * ts
    fuse = bp * heads * (sp // tq) == bs and (ts % CHUNK != 0 or ts == CHUNK)

    xp, xs = x_prompt, x_sample
    prev_p = prev_s = None
    for l in range(depth):
        lam_init = 0.8 - 0.6 * math.exp(-0.3 * l)
        lq = lambda_qk[l].astype(F32)
        lam = jnp.exp(jnp.sum(lq[0] * lq[1])) - jnp.exp(jnp.sum(lq[2] * lq[3])) + lam_init
        scal = jnp.reshape(lam, (1,)).astype(F32)
        w_gates = jnp.swapaxes(w_in[l, :, if_col:if_col + n_if], 0, 1)
        lw = dict(heads=heads, aw=aw, mw=mw, col=col, norm_pre=norm_pre[l], norm_post=norm_post[l],
                  w_t=w_t_all, w_if=jnp.pad(w_gates, ((0, LANES - n_if), (0, 0))),
                  conv_w=conv_w[l], conv_b=conv_b[l], b_if=b_if[l], mlstm_norm_w=mlstm_norm_w[l],
                  w_pa=w_pa[l].astype(BF16), w_pm=w_pm[l].astype(BF16), w_out=w_out[l].astype(BF16))
        attn_cols = dict(dqk=dqk, q_col=col["aq"], z_col=col["az"], post_scale=1.0 - lam_init)

        u2p, uifp, kp_st, vp_st = _project_in(xp, lw, layer=l, depth=depth, prev=prev_p, tm=tm_p)
        u2s, uifs, ks_st, vs_st = _project_in(xs, lw, layer=l, depth=depth, prev=prev_s, tm=tm_s)
        u_p, u_s = u2p.reshape(bp, sp, -1), u2s.reshape(bs, ts, -1)
        sample_states = (state_C, state_n, m_rep, state_conv, l)
        done_s = None
        if fuse:
            m_args, m_kw = _mlstm_operands(xs, u2s, uifs, lw, sample_states, layer=l, depth=depth, prev=prev_s)
            oa_p, oa_s, om_s, st_s = fused_attention(
                u_p, u_s, ks_st, vs_st, l, cache_k, cache_v, l, page_table, scal, attn_norm_w[l], m_args, m_kw,
                k_col=col["ak"], v_col=col["av"], tq=tq, **attn_cols)
            done_s = (om_s, st_s)
        else:
            oa_p = prompt_attention(u_p, scal, attn_norm_w[l], heads=heads, k_col=col["ak"], v_col=col["av"],
                                    tq=tq, **attn_cols)
            oa_s = sample_attention(u_s, ks_st, vs_st, l, cache_k, cache_v, l, page_table, scal, attn_norm_w[l],
                                    **attn_cols)
        xp, st_p = _finish_group(xp, u2p, uifp, oa_p, lw, zero_states + (0,), layer=l, depth=depth, prev=prev_p,
                                 chunk=CHUNK if sp % CHUNK == 0 else sp, nb=1, tm=tm_p)
        xs, st_s = _finish_group(xs, u2s, uifs, oa_s, lw, sample_states, layer=l, depth=depth, prev=prev_s,
                                 chunk=CHUNK if ts % CHUNK == 0 else ts, nb=4, tm=tm_s, mlstm_done=done_s)
        prev_p, prev_s = ((kp_st, vp_st), st_p), ((ks_st, vs_st), st_s)

    def unpack(prev, b, t):
        (k_st, v_st), (c, n, m, tail) = prev
        return (k_st.reshape(depth, b, t, heads, dh), v_st.reshape(depth, b, t, heads, dh), c, n, m[..., 0], tail)

    return (xp, xs) + unpack(prev_p, bp, sp) + unpack(prev_s, bs, ts)
```

```python
import functools
import math

import jax
import jax.numpy as jnp
import numpy as np
from jax import lax
from jax.experimental import pallas as pl
from jax.experimental.pallas import tpu as pltpu

F32 = jnp.float32
BF16 = jnp.bfloat16

EPS = 1e-6
CHUNK = 64
LANES = 128
SUBLANES = 8
BF16_ROWS = 16
NEG_BIG = -1e30
VMEM_LIMIT = 56 * 1024 * 1024
VMEM_LIMIT_FUSED = 60 * 1024 * 1024
LOG2E = math.log2(math.e)
KV_GROUP = 2


def _sigmoid(x):
    return 1.0 / (1.0 + jnp.exp(-x))


def _silu(x):
    return x * _sigmoid(x)


def _params(sem, vmem_limit=VMEM_LIMIT):
    return pltpu.CompilerParams(dimension_semantics=sem, vmem_limit_bytes=vmem_limit)


def _slabs(prev, layer, depth):
    return (depth, 0) if prev is None else (1, layer)


def _stacked_call(kernel_fn, *, grid, in_specs, inputs, out_specs, out_shape, n_stacked, prev, **kw):
    aliases = {}
    if prev is not None:
        first = len(out_shape) - n_stacked
        for k, p in enumerate(prev):
            aliases[len(inputs) + k] = first + k
        in_specs = list(in_specs) + [pl.BlockSpec(memory_space=pl.ANY)] * len(prev)
        inputs = list(inputs) + list(prev)
    return pl.pallas_call(kernel_fn, grid=grid, in_specs=in_specs, out_specs=out_specs, out_shape=out_shape,
                          input_output_aliases=aliases, **kw)(*inputs)


def _dot_nt(a, b):
    return lax.dot_general(a, b, (((1,), (1,)), ((), ())), preferred_element_type=F32)


def _inproj_kernel(x_ref, nw_ref, w_ref, wif_ref, *rest, n_prev, k_blk, v_blk, heads, nslab, slab):
    u_ref, uif_ref, k_hbm, v_hbm, h_scr, k_stage, v_stage, sems = rest[n_prev:]
    i, j = pl.program_id(0), pl.program_id(1)
    tm = x_ref.shape[0]
    dh = k_stage.shape[-1]
    rows = tm * heads
    row0 = pl.multiple_of(i * rows, rows)
    targets = ((k_blk, k_stage, k_hbm), (v_blk, v_stage, v_hbm))

    def copies(which):
        _, stage, hbm = targets[which]
        return [pltpu.make_async_copy(stage, hbm.at[slab + s, pl.ds(row0, rows), :], sems.at[which, s])
                for s in range(nslab)]

    @pl.when(j == 0)
    def _():
        x = x_ref[...]
        ms = jnp.mean(x * x, axis=-1, keepdims=True)
        h = (x * lax.rsqrt(ms + EPS) * nw_ref[...]).astype(BF16)
        h_scr[...] = h
        uif_ref[...] = _dot_nt(h, wif_ref[...].astype(BF16))

    acc = _dot_nt(h_scr[...], w_ref[...])
    u_ref[...] = acc

    for which, (blk, stage, _) in enumerate(targets):
        @pl.when(j == blk)
        def _():
            for h in range(heads):
                stage[pl.ds(h, tm, stride=heads), :] = acc[:, h * dh:(h + 1) * dh]
            for c in copies(which):
                c.start()

    @pl.when(j == pl.num_programs(1) - 1)
    def _():
        for which in range(len(targets)):
            for c in copies(which):
                c.wait()


def in_projection(x, norm_w, w_t, w_if, *, layer, depth, heads, k_col, v_col, prev_kv, tm, tn):
    t, d = x.shape
    n = w_t.shape[1]
    dh = tn // heads
    k_blk, v_blk = k_col // tn, v_col // tn
    assert k_col % tn == 0 and v_col % tn == 0 and tn == heads * dh and max(k_blk, v_blk) < n // tn - 1
    nslab, slab = _slabs(prev_kv, layer, depth)
    kern = functools.partial(_inproj_kernel, n_prev=0 if prev_kv is None else 2, k_blk=k_blk, v_blk=v_blk,
                             heads=heads, nslab=nslab, slab=slab)
    kv_spec = pl.BlockSpec(memory_space=pl.ANY)
    kv_shape = jax.ShapeDtypeStruct((depth, t * heads, dh), F32)
    stage = pltpu.VMEM((tm * heads, dh), F32)
    return _stacked_call(
        kern,
        grid=(t // tm, n // tn),
        in_specs=[
            pl.BlockSpec((tm, d), lambda i, j: (i, 0)),
            pl.BlockSpec((1, d), lambda i, j: (0, 0)),
            pl.BlockSpec((None, tn, d), lambda i, j: (layer, j, 0)),
            pl.BlockSpec((LANES, d), lambda i, j: (0, 0)),
        ],
        inputs=[x, norm_w.reshape(1, d), w_t, w_if],
        out_specs=[
            pl.BlockSpec((tm, tn), lambda i, j: (i, j)),
            pl.BlockSpec((tm, LANES), lambda i, j: (i, 0)),
            kv_spec, kv_spec,
        ],
        out_shape=[jax.ShapeDtypeStruct((t, n), F32), jax.ShapeDtypeStruct((t, LANES), F32), kv_shape, kv_shape],
        n_stacked=2, prev=prev_kv,
        scratch_shapes=[pltpu.VMEM((tm, d), BF16), stage, stage, pltpu.SemaphoreType.DMA((2, nslab))],
        compiler_params=_params(("arbitrary", "arbitrary")),
        name="in_projection",
    )


def _merge_kernel(oa_ref, om_ref, *refs, n_col):
    y_ref = refs[4 * n_col]
    tn = refs[0].shape[1]
    for c in range(n_col):
        wpa_ref, wpm_ref, ga_ref, gm_ref = refs[4 * c:4 * c + 4]
        pa = jnp.dot(oa_ref[...], wpa_ref[...], preferred_element_type=F32)
        pm = jnp.dot(om_ref[...], wpm_ref[...], preferred_element_type=F32)
        y = _sigmoid(ga_ref[...]) * pa + _sigmoid(gm_ref[...]) * pm
        y_ref[:, c * tn:(c + 1) * tn] = y.astype(y_ref.dtype)


def gated_merge(o_a, o_m, w_pa, w_pm, u, ga_col, gm_col, *, tm, tn):
    t, wa = o_a.shape
    wm = o_m.shape[1]
    d = w_pa.shape[1]
    assert ga_col % tn == 0 and gm_col % tn == 0 and d % tn == 0
    n_col = d // tn
    specs, args = [], []
    for c in range(n_col):
        specs += [pl.BlockSpec((wa, tn), lambda i, c=c: (0, c)),
                  pl.BlockSpec((wm, tn), lambda i, c=c: (0, c)),
                  pl.BlockSpec((tm, tn), lambda i, c=c: (i, ga_col // tn + c)),
                  pl.BlockSpec((tm, tn), lambda i, c=c: (i, gm_col // tn + c))]
        args += [w_pa, w_pm, u, u]
    return pl.pallas_call(
        functools.partial(_merge_kernel, n_col=n_col),
        grid=(t // tm,),
        in_specs=[pl.BlockSpec((tm, wa), lambda i: (i, 0)), pl.BlockSpec((tm, wm), lambda i: (i, 0))] + specs,
        out_specs=pl.BlockSpec((tm, d), lambda i: (i, 0)),
        out_shape=jax.ShapeDtypeStruct((t, d), BF16),
        compiler_params=_params(("parallel",)),
        name="gated_merge",
    )(o_a, o_m, *args)


def _outproj_kernel(y_ref, w_ref, nw_ref, x_ref, o_ref):
    out = jnp.dot(y_ref[...], w_ref[...], preferred_element_type=F32)
    ms = jnp.mean(out * out, axis=-1, keepdims=True)
    o_ref[...] = x_ref[...] + out * lax.rsqrt(ms + EPS) * nw_ref[...]


def out_projection(y, w_out, norm_w, x, *, tm):
    t, d = x.shape
    return pl.pallas_call(
        _outproj_kernel,
        grid=(t // tm,),
        in_specs=[
            pl.BlockSpec((tm, d), lambda i: (i, 0)),
            pl.BlockSpec((d, d), lambda i: (0, 0)),
            pl.BlockSpec((1, d), lambda i: (0, 0)),
            pl.BlockSpec((tm, d), lambda i: (i, 0)),
        ],
        out_specs=pl.BlockSpec((tm, d), lambda i: (i, 0)),
        out_shape=jax.ShapeDtypeStruct((t, d), F32),
        compiler_params=_params(("parallel",)),
        name="out_projection",
    )(y, w_out, norm_w.reshape(1, d), x)


def _prompt_attn_kernel(*refs, **kw):
    _prompt_attn_body(pl.program_id(2), *refs, **kw)


def _prompt_attn_body(qi, scal_ref, q_ref, k_ref, v_ref, z_ref, nw_ref, o_ref, kb_scr, vt_scr,
                      *, tq, dqk, scale_log2, post_scale):
    lam = scal_ref[0]
    dv = v_ref.shape[-1]
    n_blk = vt_scr.shape[0]

    @pl.when(qi == 0)
    def _():
        kb_scr[...] = k_ref[0].astype(BF16)
        for c in range(n_blk):
            vt_scr[c, 0:dv, :] = jnp.transpose(v_ref[0, c * tq:(c + 1) * tq, :]).astype(BF16)
            vt_scr[c, dv:, :] = jnp.ones((BF16_ROWS, tq), BF16)

    q = q_ref[0] * scale_log2
    lane = lax.broadcasted_iota(jnp.int32, q.shape, 1)
    q1 = jnp.where(lane < dqk, q, 0.0).astype(BF16)
    q2 = jnp.where(lane >= dqk, q, 0.0).astype(BF16)

    def scores_t(k, qm):
        return lax.dot_general(k, qm, (((1,), (1,)), ((), ())), preferred_element_type=F32)

    def online(state, s_t, vt):
        m, acc = state
        m_new = jnp.maximum(m, jnp.max(s_t, axis=0, keepdims=True))
        alpha = jnp.exp2(m - m_new)
        p = jnp.exp2(s_t - m_new)
        acc = alpha * acc + jnp.dot(vt, p.astype(BF16), preferred_element_type=F32)
        return m_new, acc

    def scores_of(j):
        k = kb_scr[pl.ds(pl.multiple_of(j * tq, tq), tq), :]
        return scores_t(k, q1), scores_t(k, q2)

    def causal(s):
        krow = lax.broadcasted_iota(jnp.int32, s.shape, 0)
        qcol = lax.broadcasted_iota(jnp.int32, s.shape, 1)
        return jnp.where(krow <= qcol, s, -jnp.inf)

    def group(j, carry, n, last_is_diagonal):
        st1, st2 = carry
        scores = [scores_of(j + i) for i in range(n)]
        for i, (s1, s2) in enumerate(scores):
            if last_is_diagonal and i == n - 1:
                s1, s2 = causal(s1), causal(s2)
            vt = vt_scr[j + i]
            st1, st2 = online(st1, s1, vt), online(st2, s2, vt)
        return st1, st2

    def init():
        return jnp.full((1, tq), NEG_BIG, F32), jnp.zeros((dv + BF16_ROWS, tq), F32)

    n_full = qi // KV_GROUP
    carry = lax.fori_loop(0, n_full, lambda g, c: group(KV_GROUP * g, c, KV_GROUP, False), (init(), init()))
    tails = [functools.partial(group, n=r + 1, last_is_diagonal=True) for r in range(KV_GROUP)]
    rest0 = n_full * KV_GROUP
    (_, a1), (_, a2) = lax.switch(qi - rest0, [lambda c, f=f: f(rest0, c) for f in tails], carry)

    out_t = a1[0:dv] / a1[dv:dv + 1] - lam * (a2[0:dv] / a2[dv:dv + 1])
    out = jnp.transpose(out_t)
    ms = jnp.mean(out * out, axis=-1, keepdims=True)
    y = out * lax.rsqrt(ms + EPS) * nw_ref[...] * post_scale
    o_ref[0] = (y * _silu(z_ref[0])).astype(o_ref.dtype)


def prompt_attention(u, scal, norm_w, *, heads, dqk, q_col, k_col, v_col, z_col, post_scale, tq):
    b, s, _ = u.shape
    dh = 2 * dqk
    qb, kb, vb, zb = q_col // dh, k_col // dh, v_col // dh, z_col // dh
    kern = functools.partial(_prompt_attn_kernel, tq=tq, dqk=dqk, scale_log2=dqk ** -0.5 * LOG2E,
                             post_scale=post_scale)
    return pl.pallas_call(
        kern,
        grid=(b, heads, s // tq),
        in_specs=[
            pl.BlockSpec(memory_space=pltpu.SMEM),
            pl.BlockSpec((1, tq, dh), lambda bi, h, i: (bi, i, qb + h)),
            pl.BlockSpec((1, s, dh), lambda bi, h, i: (bi, 0, kb + h)),
            pl.BlockSpec((1, s, dh), lambda bi, h, i: (bi, 0, vb + h)),
            pl.BlockSpec((1, tq, dh), lambda bi, h, i: (bi, i, zb + h)),
            pl.BlockSpec((1, dh), lambda bi, h, i: (0, h)),
        ],
        out_specs=pl.BlockSpec((1, tq, dh), lambda bi, h, i: (bi, i, h)),
        out_shape=jax.ShapeDtypeStruct((b, s, heads * dh), BF16),
        scratch_shapes=[pltpu.VMEM((s, dh), BF16), pltpu.VMEM((s // tq, dh + BF16_ROWS, tq), BF16)],
        compiler_params=_params(("parallel", "parallel", "arbitrary")),
        name="prompt_attention",
    )(scal, u, u, u, u, norm_w.reshape(1, heads * dh))


def _sample_attn_kernel(pt_ref, scal_ref, q_ref, kn_ref, vn_ref, z_ref, nw_ref, bias_ref, biasn_ref, *rest,
                        n_pages, **kw):
    del pt_ref
    _sample_attn_body(scal_ref, q_ref, kn_ref, vn_ref, z_ref, nw_ref, bias_ref, biasn_ref,
                      rest[:n_pages], rest[n_pages:2 * n_pages], rest[2 * n_pages], rest[2 * n_pages + 1], **kw)


def _sample_attn_body(scal_ref, q_ref, kn_ref, vn_ref, z_ref, nw_ref, bias_ref, biasn_ref, k_refs, v_refs,
                      o_ref, s_scr, *, heads, dqk, nq, scale, post_scale):
    n_pages = len(k_refs)
    lam = scal_ref[0]
    dh = 2 * dqk
    rows_pp = k_refs[0].shape[0] * k_refs[0].shape[1]

    q = q_ref[0] * (scale * LOG2E)
    lane = lax.broadcasted_iota(jnp.int32, (nq, dh), 1)
    pieces = []
    for h in range(heads):
        qh = q[:, h * dh:(h + 1) * dh]
        pieces.append(jnp.where(lane < dqk, qh, 0.0))
        pieces.append(jnp.where(lane >= dqk, qh, 0.0))
    qall = jnp.concatenate(pieces, axis=0).astype(BF16)
    nrow = qall.shape[0]

    def scores(k2d):
        return lax.dot_general(qall, k2d, (((1,), (1,)), ((), ())), preferred_element_type=F32)

    bias = bias_ref[...]
    slabs = [slice(c * LANES, (c + 1) * LANES) for c in range(rows_pp // LANES)]
    m_acc = jnp.full((nrow, LANES), NEG_BIG, F32)
    for j in range(n_pages):
        kj = k_refs[j][...].reshape(rows_pp, dh).astype(BF16)
        s = scores(kj)
        for c in slabs:
            sc = s[:, c] + bias
            s_scr[:, j * rows_pp + c.start:j * rows_pp + c.stop] = sc
            m_acc = jnp.maximum(m_acc, sc)
    sn = scores(kn_ref[...].astype(BF16)) + biasn_ref[...]
    m = jnp.maximum(jnp.max(m_acc, axis=-1, keepdims=True), jnp.max(sn, axis=-1, keepdims=True))

    l_acc = jnp.zeros((nrow, LANES), F32)
    acc = jnp.zeros((nrow, dh), F32)
    for j in range(n_pages):
        p = jnp.exp2(s_scr[:, j * rows_pp:(j + 1) * rows_pp] - m)
        for c in slabs:
            l_acc = l_acc + p[:, c]
        vj = v_refs[j][...].reshape(rows_pp, dh).astype(BF16)
        acc = acc + jnp.dot(p.astype(BF16), vj, preferred_element_type=F32)
    pn = jnp.exp2(sn - m)
    l = jnp.sum(l_acc, axis=-1, keepdims=True) + jnp.sum(pn, axis=-1, keepdims=True)
    acc = acc + jnp.dot(pn.astype(BF16), vn_ref[...].astype(BF16), preferred_element_type=F32)

    o = acc / l
    z = z_ref[0]
    nw = nw_ref[...]
    for h in range(heads):
        r0 = h * 2 * nq
        oh = o[r0:r0 + nq] - lam * o[r0 + nq:r0 + 2 * nq]
        ms = jnp.mean(oh * oh, axis=-1, keepdims=True)
        y = oh * lax.rsqrt(ms + EPS) * nw[:, h * dh:(h + 1) * dh] * post_scale
        o_ref[0, :, h * dh:(h + 1) * dh] = (y * _silu(z[:, h * dh:(h + 1) * dh])).astype(o_ref.dtype)


def _sample_masks(heads, nq, page_rows):
    r = np.arange(heads * 2 * nq)
    rh, rt = r // (2 * nq), r % nq
    c = np.arange(page_rows)
    past = np.where((c[None, :] % heads) == rh[:, None], 0.0, -np.inf).astype(np.float32)
    cn = np.arange(nq * heads)
    ok = ((cn[None, :] % heads) == rh[:, None]) & ((cn[None, :] // heads) <= rt[:, None])
    new = np.where(ok, 0.0, -np.inf).astype(np.float32)
    return jnp.asarray(past), jnp.asarray(new)


def sample_attention(u, k_rows, v_rows, kv_layer, cache_k, cache_v, layer, page_table, scal, norm_w,
                     *, dqk, q_col, z_col, post_scale):
    b, nq, _ = u.shape
    _, _, page, heads, dh = cache_k.shape
    n_pages = page_table.shape[1]
    width = heads * dh
    qb, zb = q_col // width, z_col // width
    assert LANES % heads == 0 and (page * heads) % LANES == 0
    bias, bias_new = _sample_masks(heads, nq, LANES)
    nrow = heads * 2 * nq

    def page_spec(j):
        return pl.BlockSpec((None, None, page, heads, dh), lambda i, pt: (layer, pt[i, j], 0, 0, 0))

    new_spec = pl.BlockSpec((None, nq * heads, dh), lambda i, pt: (kv_layer, i, 0))
    kern = functools.partial(_sample_attn_kernel, n_pages=n_pages, heads=heads, dqk=dqk, nq=nq,
                             scale=dqk ** -0.5, post_scale=post_scale)
    grid_spec = pltpu.PrefetchScalarGridSpec(
        num_scalar_prefetch=1,
        grid=(b,),
        in_specs=[
            pl.BlockSpec(memory_space=pltpu.SMEM),
            pl.BlockSpec((1, nq, width), lambda i, pt: (i, 0, qb)),
            new_spec, new_spec,
            pl.BlockSpec((1, nq, width), lambda i, pt: (i, 0, zb)),
            pl.BlockSpec((1, width), lambda i, pt: (0, 0)),
            pl.BlockSpec(bias.shape, lambda i, pt: (0, 0)),
            pl.BlockSpec(bias_new.shape, lambda i, pt: (0, 0)),
        ] + [page_spec(j) for j in range(n_pages)] * 2,
        out_specs=pl.BlockSpec((1, nq, width), lambda i, pt: (i, 0, 0)),
        scratch_shapes=[pltpu.VMEM((nrow, n_pages * page * heads), F32)],
    )
    return pl.pallas_call(
        kern,
        grid_spec=grid_spec,
        out_shape=jax.ShapeDtypeStruct((b, nq, width), BF16),
        compiler_params=_params(("arbitrary",)),
        name="sample_attention",
    )(page_table, scal, u, k_rows, v_rows, u, norm_w.reshape(1, width), bias, bias_new,
      *([cache_k] * n_pages), *([cache_v] * n_pages))


def _fused_attn_kernel(pt_ref, scal_ref, qp_ref, kp_ref, vp_ref, zp_ref, nwp_ref,
                       qs_ref, kn_ref, vn_ref, zs_ref, nws_ref, bias_ref, biasn_ref, *rest,
                       n_pages, n_qblk, heads, dqk, nq, tq, post_scale, n_prev, m_heads, taps):
    del pt_ref
    k_refs, v_refs = rest[:n_pages], rest[n_pages:2 * n_pages]
    rest = rest[2 * n_pages:]
    mlstm_in, rest = rest[:13], rest[13 + n_prev:]
    op_ref, os_ref = rest[:2]
    mlstm_out = rest[2:7]
    kb_scr, vt_scr, s_scr, ext_scr, m_scr = rest[7:]
    _sample_attn_body(scal_ref, qs_ref, kn_ref, vn_ref, zs_ref, nws_ref, bias_ref, biasn_ref, k_refs, v_refs,
                      os_ref, s_scr, heads=heads, dqk=dqk, nq=nq, scale=dqk ** -0.5, post_scale=post_scale)
    _mlstm_body(0, 1, *mlstm_in, *mlstm_out, ext_scr, m_scr, heads=m_heads, taps=taps, single_chunk=True)
    _prompt_attn_body(pl.program_id(0) % n_qblk, scal_ref, qp_ref, kp_ref, vp_ref, zp_ref, nwp_ref, op_ref,
                      kb_scr, vt_scr, tq=tq, dqk=dqk, scale_log2=dqk ** -0.5 * LOG2E, post_scale=post_scale)


def fused_attention(u_p, u_s, k_rows, v_rows, kv_layer, cache_k, cache_v, layer, page_table, scal, norm_w,
                    mlstm_args, mlstm_kw, *, dqk, q_col, k_col, v_col, z_col, post_scale, tq):
    bp, sp, _ = u_p.shape
    bs, nq, _ = u_s.shape
    _, _, page, heads, dh = cache_k.shape
    n_pages = page_table.shape[1]
    width = heads * dh
    n_qblk = sp // tq
    assert bp * heads * n_qblk == bs
    qb, kb, vb, zb = q_col // dh, k_col // dh, v_col // dh, z_col // dh
    qbs, zbs = q_col // width, z_col // width
    assert LANES % heads == 0 and (page * heads) % LANES == 0
    bias, bias_new = _sample_masks(heads, nq, LANES)

    def pb(s):
        return s // (heads * n_qblk)

    def ph(s):
        return (s // n_qblk) % heads

    def pq(s):
        return s % n_qblk

    def page_spec(j):
        return pl.BlockSpec((None, None, page, heads, dh), lambda s, pt: (layer, pt[s, j], 0, 0, 0))

    new_spec = pl.BlockSpec((None, nq * heads, dh), lambda s, pt: (kv_layer, s, 0))
    prev = mlstm_kw["prev"]
    mio = _mlstm_io(*mlstm_args, **mlstm_kw, chunk=nq, nb=1, wrap=lambda f: (lambda s, pt: f(s, 0)))
    n_prev = 0 if prev is None else len(prev)
    kern = functools.partial(_fused_attn_kernel, n_pages=n_pages, n_qblk=n_qblk, heads=heads, dqk=dqk, nq=nq,
                             tq=tq, post_scale=post_scale, n_prev=n_prev, m_heads=mlstm_args[6].shape[2],
                             taps=mlstm_args[2].shape[0])
    nw2 = norm_w.reshape(1, width)
    in_specs = [
        pl.BlockSpec(memory_space=pltpu.SMEM),
        pl.BlockSpec((1, tq, dh), lambda s, pt: (pb(s), pq(s), qb + ph(s))),
        pl.BlockSpec((1, sp, dh), lambda s, pt: (pb(s), 0, kb + ph(s))),
        pl.BlockSpec((1, sp, dh), lambda s, pt: (pb(s), 0, vb + ph(s))),
        pl.BlockSpec((1, tq, dh), lambda s, pt: (pb(s), pq(s), zb + ph(s))),
        pl.BlockSpec((1, dh), lambda s, pt: (0, ph(s))),
        pl.BlockSpec((1, nq, width), lambda s, pt: (s, 0, qbs)),
        new_spec, new_spec,
        pl.BlockSpec((1, nq, width), lambda s, pt: (s, 0, zbs)),
        pl.BlockSpec((1, width), lambda s, pt: (0, 0)),
        pl.BlockSpec(bias.shape, lambda s, pt: (0, 0)),
        pl.BlockSpec(bias_new.shape, lambda s, pt: (0, 0)),
    ] + [page_spec(j) for j in range(n_pages)] * 2 + mio["in_specs"]
    inputs = [scal, u_p, u_p, u_p, u_p, nw2, u_s, k_rows, v_rows, u_s, nw2, bias, bias_new,
              *([cache_k] * n_pages), *([cache_v] * n_pages)] + mio["inputs"]
    aliases = {}
    if prev is not None:
        for k, p in enumerate(prev):
            aliases[1 + len(inputs) + k] = 3 + k
        in_specs = in_specs + [pl.BlockSpec(memory_space=pl.ANY)] * n_prev
        inputs = inputs + list(prev)
    grid_spec = pltpu.PrefetchScalarGridSpec(
        num_scalar_prefetch=1,
        grid=(bs,),
        in_specs=in_specs,
        out_specs=[
            pl.BlockSpec((1, tq, dh), lambda s, pt: (pb(s), pq(s), ph(s))),
            pl.BlockSpec((1, nq, width), lambda s, pt: (s, 0, 0)),
        ] + mio["out_specs"],
        scratch_shapes=[pltpu.VMEM((sp, dh), BF16), pltpu.VMEM((n_qblk, dh + BF16_ROWS, tq), BF16),
                        pltpu.VMEM((heads * 2 * nq, n_pages * page * heads), F32)] + mio["scratch"],
    )
    res = pl.pallas_call(
        kern,
        grid_spec=grid_spec,
        out_shape=[jax.ShapeDtypeStruct((bp, sp, width), BF16), jax.ShapeDtypeStruct((bs, nq, width), BF16)]
        + mio["out_shape"],
        input_output_aliases=aliases,
        compiler_params=_params(("arbitrary",), VMEM_LIMIT_FUSED),
        name="fused_attention",
    )(page_table, *inputs)
    return res[0], res[1], res[2], tuple(res[3:])


def _mlstm_kernel(*refs, n_prev, **kw):
    _mlstm_body(pl.program_id(1), pl.num_programs(1), *refs[:13], *refs[13 + n_prev:], **kw)


def _mlstm_body(ci, nc, qk_ref, v_ref, og_ref, z_ref, if_ref, cw_ref, cb_ref, bif_ref, nw_ref,
                c0_ref, n0_ref, m0_ref, buf0_ref, om_ref, c_ref, n_ref, m_ref, tail_ref, ext_scr, m_scr,
                *, heads, taps, single_chunk):
    nb, L = qk_ref.shape[0], qk_ref.shape[1]
    mw = v_ref.shape[2]
    dh = mw // heads
    pad = SUBLANES
    nslab = c_ref.shape[0]

    if single_chunk:
        for bb in range(nb):
            ext_scr[bb, pad - (taps - 1):pad, :] = buf0_ref[bb]
    else:
        @pl.when(ci == 0)
        def _():
            c_ref[0] = c0_ref[...]
            n_ref[0] = n0_ref[...]
            for bb in range(nb):
                for h in range(heads):
                    m_scr[bb * heads + h] = jnp.broadcast_to(m0_ref[bb, h:h + 1, :], m_scr.shape[1:])
                ext_scr[bb, pad - (taps - 1):pad, :] = buf0_ref[bb]

    row = lax.broadcasted_iota(jnp.int32, (L, L), 0)
    col = lax.broadcasted_iota(jnp.int32, (L, L), 1)
    eye = row == col
    low = col <= row

    for bb in range(nb):
        ext_scr[bb, pad:pad + L, :] = qk_ref[bb]
        conv = cb_ref[...]
        for j in range(taps):
            r0 = pad - (taps - 1) + j
            conv = conv + ext_scr[bb, r0:r0 + L, :] * cw_ref[j:j + 1, :]

        ext_scr[bb, 0:pad, :] = ext_scr[bb, L:L + pad, :]
        qk = _silu(conv)
        q_all = qk[:, :mw]
        k_all = qk[:, mw:] * (dh ** -0.5)
        v_all = v_ref[bb]
        g = if_ref[bb] + bif_ref[...]
        lf_all = jnp.minimum(g, 0.0) - jnp.log(1.0 + jnp.exp(-jnp.abs(g)))
        for h in range(heads):
            li_col = g[:, h:h + 1]
            lf_col = lf_all[:, heads + h:heads + h + 1]
            lf_row = jnp.sum(jnp.where(eye, lf_col, 0.0), axis=0, keepdims=True)
            bt_col = jnp.sum(jnp.where(low, lf_row, 0.0), axis=1, keepdims=True)
            r_row = jnp.sum(jnp.where(eye, li_col - bt_col, 0.0), axis=0, keepdims=True)
            m_prev = m0_ref[bb, h:h + 1, 0:1] if single_chunk else m_scr[bb * heads + h, 0:1, 0:1]

            log_d = jnp.where(low, bt_col + r_row, -jnp.inf)
            log_inter = bt_col + m_prev
            m_t = jnp.maximum(log_inter, jnp.max(log_d, axis=1, keepdims=True))
            d_mat = jnp.exp(log_d - m_t)
            inter = jnp.exp(log_inter - m_t)

            sl = slice(h * dh, (h + 1) * dh)
            q_h, k_h = q_all[:, sl], k_all[:, sl]
            q_b, k_b, v_b = q_h.astype(BF16), k_h.astype(BF16), v_all[:, sl].astype(BF16)
            c_h = c0_ref[bb, h] if single_chunk else c_ref[0, bb, h]
            n_h = n0_ref[bb, h:h + 1, :] if single_chunk else n_ref[0, bb, h:h + 1, :]

            s = lax.dot_general(q_b, k_b, (((1,), (1,)), ((), ())), preferred_element_type=F32) * d_mat
            num = (jnp.dot(s.astype(BF16), v_b, preferred_element_type=F32)
                   + jnp.dot(q_b, c_h.astype(BF16), preferred_element_type=F32) * inter)
            den = jnp.sum(s, axis=1, keepdims=True) + inter * jnp.sum(q_h * n_h, axis=1, keepdims=True)
            den = jnp.maximum(jnp.abs(den), jnp.exp(-m_t))
            hh = num / den

            b_last = jnp.sum(lf_col, axis=0, keepdims=True)
            log_w = b_last - bt_col + li_col
            m_new = jnp.maximum(b_last + m_prev, jnp.max(log_w, axis=0, keepdims=True))
            wk = jnp.exp(log_w - m_new)
            decay = jnp.exp(b_last + m_prev - m_new)
            kw = k_h * wk
            upd = lax.dot_general(kw.astype(BF16), v_b, (((0,), (0,)), ((), ())), preferred_element_type=F32)
            c_new = decay * c_h + upd
            n_new = decay * n_h + jnp.sum(kw, axis=0, keepdims=True)
            for s in range(nslab if single_chunk else 1):
                c_ref[s, bb, h] = c_new
                n_ref[s, bb, h:h + 1, :] = n_new
            m_scr[bb * heads + h] = jnp.broadcast_to(m_new, m_scr.shape[1:])

            mu = jnp.mean(hh, axis=1, keepdims=True)
            xc = hh - mu
            var = jnp.mean(xc * xc, axis=1, keepdims=True)
            y = xc * lax.rsqrt(var + EPS) * nw_ref[:, sl] * _sigmoid(og_ref[bb, :, sl])
            om_ref[bb, :, sl] = (y * _silu(z_ref[bb, :, sl])).astype(om_ref.dtype)

    def finish():
        for s in range(nslab):
            if s > 0 and not single_chunk:
                c_ref[s] = c_ref[0]
                n_ref[s] = n_ref[0]
            for bb in range(nb):
                tail_ref[s, bb] = ext_scr[bb, pad - (taps - 1):pad, :]
                for h in range(heads):
                    m_ref[s, bb, h:h + 1, :] = m_scr[bb * heads + h, 0:1, :]

    if single_chunk:
        finish()
    else:
        pl.when(ci == nc - 1)(finish)


def mlstm_branch(u, u_if, conv_w, conv_b, b_if, norm_w, c0, n0, m0, buf0, *, state_layer, layer, depth, prev,
                 chunk, nb, qk_col, v_col, og_col, z_col):
    b, t, _ = u.shape
    heads = c0.shape[2]
    taps = conv_w.shape[0]
    io = _mlstm_io(u, u_if, conv_w, conv_b, b_if, norm_w, c0, n0, m0, buf0, state_layer=state_layer, layer=layer,
                   depth=depth, prev=prev, chunk=chunk, nb=nb, qk_col=qk_col, v_col=v_col, og_col=og_col,
                   z_col=z_col, wrap=lambda f: f)
    kern = functools.partial(_mlstm_kernel, n_prev=0 if prev is None else 4, heads=heads, taps=taps,
                             single_chunk=(t == chunk))
    res = _stacked_call(
        kern, grid=(b // nb, t // chunk), in_specs=io["in_specs"], inputs=io["inputs"], out_specs=io["out_specs"],
        out_shape=io["out_shape"], n_stacked=4, prev=prev, scratch_shapes=io["scratch"],
        compiler_params=_params(("parallel", "arbitrary")), name="mlstm_branch")
    return res[0], tuple(res[1:])


def _mlstm_io(u, u_if, conv_w, conv_b, b_if, norm_w, c0, n0, m0, buf0, *, state_layer, layer, depth, prev,
              chunk, nb, qk_col, v_col, og_col, z_col, wrap):
    b, t, _ = u.shape
    _, _, heads, dh, _ = c0.shape
    mw = heads * dh
    taps = conv_w.shape[0]
    assert chunk >= taps - 1 and t % chunk == 0 and b % nb == 0
    bif_pad = jnp.zeros((1, LANES), F32).at[0, :2 * heads].set(b_if)
    nslab, slab = _slabs(prev, layer, depth)

    def spec(shape, f):
        return pl.BlockSpec(shape, wrap(f))

    const2 = lambda bi, ci: (0, 0)
    st_in = lambda bi, ci: (state_layer, bi, 0, 0)
    st_out = lambda bi, ci: (slab, bi, 0, 0)
    return dict(
        in_specs=[
            spec((nb, chunk, 2 * mw), lambda bi, ci: (bi, ci, qk_col // (2 * mw))),
            spec((nb, chunk, mw), lambda bi, ci: (bi, ci, v_col // mw)),
            spec((nb, chunk, mw), lambda bi, ci: (bi, ci, og_col // mw)),
            spec((nb, chunk, mw), lambda bi, ci: (bi, ci, z_col // mw)),
            spec((nb, chunk, LANES), lambda bi, ci: (bi, ci, 0)),
            spec((taps, 2 * mw), const2),
            spec((1, 2 * mw), const2),
            spec((1, LANES), const2),
            spec((1, mw), const2),
            spec((None, nb, heads, dh, dh), lambda bi, ci: (state_layer, bi, 0, 0, 0)),
            spec((None, nb, heads, dh), st_in),
            spec((None, nb, heads, LANES), st_in),
            spec((None, nb, taps - 1, 2 * mw), st_in),
        ],
        inputs=[u, u, u, u, u_if, conv_w, conv_b.reshape(1, 2 * mw), bif_pad, norm_w.reshape(1, mw),
                c0, n0, m0, buf0],
        out_specs=[
            spec((nb, chunk, mw), lambda bi, ci: (bi, ci, 0)),
            spec((nslab, nb, heads, dh, dh), lambda bi, ci: (slab, bi, 0, 0, 0)),
            spec((nslab, nb, heads, dh), st_out),
            spec((nslab, nb, heads, LANES), st_out),
            spec((nslab, nb, taps - 1, 2 * mw), st_out),
        ],
        out_shape=[
            jax.ShapeDtypeStruct((b, t, mw), BF16),
            jax.ShapeDtypeStruct((depth, b, heads, dh, dh), F32),
            jax.ShapeDtypeStruct((depth, b, heads, dh), F32),
            jax.ShapeDtypeStruct((depth, b, heads, LANES), F32),
            jax.ShapeDtypeStruct((depth, b, taps - 1, 2 * mw), F32),
        ],
        scratch=[pltpu.VMEM((nb, chunk + SUBLANES, 2 * mw), F32),
                 pltpu.VMEM((nb * heads, SUBLANES, LANES), F32)],
    )


def _wprep_kernel(w_hbm, o_ref, buf, sems, *, n_head_blk, skip):
    l, k = pl.program_id(0), pl.program_id(1)
    nk = pl.num_programs(1)
    tn = o_ref.shape[0]

    def copy(kk, slot):
        off = pl.multiple_of(kk * tn + jnp.where(kk >= n_head_blk, skip, 0), SUBLANES)
        return pltpu.make_async_copy(w_hbm.at[l, pl.ds(off, tn), :], buf.at[slot], sems.at[slot])

    slot = k % 2

    @pl.when(k == 0)
    def _():
        copy(k, slot).start()

    @pl.when(k + 1 < nk)
    def _():
        copy(k + 1, 1 - slot).start()

    copy(k, slot).wait()
    o_ref[...] = buf[slot].astype(o_ref.dtype)


def prepare_in_weight(w_t_all, *, head_rows, skip, tn):
    depth, p_in, d = w_t_all.shape
    n = p_in - skip
    assert head_rows % tn == 0 and n % tn == 0 and skip % SUBLANES == 0
    kern = functools.partial(_wprep_kernel, n_head_blk=head_rows // tn, skip=skip)
    return pl.pallas_call(
        kern,
        grid=(depth, n // tn),
        in_specs=[pl.BlockSpec(memory_space=pl.ANY)],
        out_specs=pl.BlockSpec((None, tn, d), lambda l, k: (l, k, 0)),
        out_shape=jax.ShapeDtypeStruct((depth, n, d), BF16),
        scratch_shapes=[pltpu.VMEM((2, tn, d), F32), pltpu.SemaphoreType.DMA((2,))],
        compiler_params=_params(("arbitrary", "arbitrary")),
        name="prepare_in_weight",
    )(w_t_all)


def _project_in(x, lw, *, layer, depth, prev, tm):
    b, t, d = x.shape
    col = lw["col"]
    return in_projection(
        x.reshape(b * t, d), lw["norm_pre"], lw["w_t"], lw["w_if"], layer=layer, depth=depth,
        heads=lw["heads"], k_col=col["ak"], v_col=col["av"], prev_kv=None if prev is None else prev[0],
        tm=tm, tn=lw["aw"])


def _mlstm_operands(x, u2, uif2, lw, states, *, layer, depth, prev):
    b, t, _ = x.shape
    col = lw["col"]
    c0, n0, m0, buf0, state_layer = states
    args = (u2.reshape(b, t, -1), uif2.reshape(b, t, LANES), lw["conv_w"], lw["conv_b"], lw["b_if"],
            lw["mlstm_norm_w"], c0, n0, m0, buf0)
    kw = dict(state_layer=state_layer, layer=layer, depth=depth, prev=None if prev is None else prev[1],
              qk_col=col["mq"], v_col=col["mv"], og_col=col["mo"], z_col=col["mz"])
    return args, kw


def _finish_group(x, u2, uif2, o_a, lw, states, *, layer, depth, prev, chunk, nb, tm, mlstm_done=None):
    b, t, d = x.shape
    aw, mw, col = lw["aw"], lw["mw"], lw["col"]
    if mlstm_done is None:
        args, kw = _mlstm_operands(x, u2, uif2, lw, states, layer=layer, depth=depth, prev=prev)
        o_m, st = mlstm_branch(*args, **kw, chunk=chunk, nb=nb)
    else:
        o_m, st = mlstm_done
    y = gated_merge(o_a.reshape(b * t, aw), o_m.reshape(b * t, mw), lw["w_pa"], lw["w_pm"], u2,
                    col["ga"], col["gm"], tm=min(tm, 512), tn=1024)
    x_new = out_projection(y, lw["w_out"], lw["norm_post"], x.reshape(b * t, d), tm=min(tm, 512)).reshape(b, t, d)
    return x_new, st


def kernel(x_prompt, x_sample, cache_k, cache_v, state_C, state_n, state_m, state_conv, page_table,
           norm_pre, norm_post, w_in, b_if, conv_w, conv_b, lambda_qk, attn_norm_w, mlstm_norm_w,
           w_pa, w_pm, w_out):
    depth = w_in.shape[0]
    d_model = x_prompt.shape[-1]
    heads = cache_k.shape[3]
    dqk = lambda_qk.shape[-1]
    dh = 2 * dqk
    aw = heads * dh
    m_heads, m_dh = state_C.shape[2], state_C.shape[3]
    mw = m_heads * m_dh
    bp, sp, _ = x_prompt.shape
    bs, ts, _ = x_sample.shape
    taps = conv_w.shape[1]
    n_if = 2 * m_heads
    if_col = 4 * aw + 4 * mw
    col = {"aq": 0, "ak": aw, "av": 2 * aw, "az": 3 * aw, "mq": 4 * aw, "mv": 4 * aw + 2 * mw,
           "mo": 4 * aw + 3 * mw, "mz": if_col, "ga": if_col + mw, "gm": if_col + mw + d_model}

    zero_states = (jnp.zeros((1, bp, m_heads, m_dh, m_dh), F32), jnp.zeros((1, bp, m_heads, m_dh), F32),
                   jnp.zeros((1, bp, m_heads, LANES), F32), jnp.zeros((1, bp, taps - 1, 2 * mw), F32))
    m_rep = jnp.broadcast_to(state_m[..., None], state_m.shape + (LANES,))

    w_t_all = prepare_in_weight(jnp.swapaxes(w_in, 1, 2), head_rows=if_col, skip=n_if, tn=aw)
    tq = 512
    tm_p, tm_s = min(1024, bp * sp), bs * ts
    fuse = bp * heads * (sp // tq) == bs and (ts % CHUNK != 0 or ts == CHUNK)

    xp, xs = x_prompt, x_sample
    prev_p = prev_s = None
    for l in range(depth):
        lam_init = 0.8 - 0.6 * math.exp(-0.3 * l)
        lq = lambda_qk[l].astype(F32)
        lam = jnp.exp(jnp.sum(lq[0] * lq[1])) - jnp.exp(jnp.sum(lq[2] * lq[3])) + lam_init
        scal = jnp.reshape(lam, (1,)).astype(F32)
        w_gates = jnp.swapaxes(w_in[l, :, if_col:if_col + n_if], 0, 1)
        lw = dict(heads=heads, aw=aw, mw=mw, col=col, norm_pre=norm_pre[l], norm_post=norm_post[l],
                  w_t=w_t_all, w_if=jnp.pad(w_gates, ((0, LANES - n_if), (0, 0))),
                  conv_w=conv_w[l], conv_b=conv_b[l], b_if=b_if[l], mlstm_norm_w=mlstm_norm_w[l],
                  w_pa=w_pa[l].astype(BF16), w_pm=w_pm[l].astype(BF16), w_out=w_out[l].astype(BF16))
        attn_cols = dict(dqk=dqk, q_col=col["aq"], z_col=col["az"], post_scale=1.0 - lam_init)

        u2p, uifp, kp_st, vp_st = _project_in(xp, lw, layer=l, depth=depth, prev=prev_p, tm=tm_p)
        u2s, uifs, ks_st, vs_st = _project_in(xs, lw, layer=l, depth=depth, prev=prev_s, tm=tm_s)
        u_p, u_s = u2p.reshape(bp, sp, -1), u2s.reshape(bs, ts, -1)
        sample_states = (state_C, state_n, m_rep, state_conv, l)
        done_s = None
        if fuse:
            m_args, m_kw = _mlstm_operands(xs, u2s, uifs, lw, sample_states, layer=l, depth=depth, prev=prev_s)
            oa_p, oa_s, om_s, st_s = fused_attention(
                u_p, u_s, ks_st, vs_st, l, cache_k, cache_v, l, page_table, scal, attn_norm_w[l], m_args, m_kw,
                k_col=col["ak"], v_col=col["av"], tq=tq, **attn_cols)
            done_s = (om_s, st_s)
        else:
            oa_p = prompt_attention(u_p, scal, attn_norm_w[l], heads=heads, k_col=col["ak"], v_col=col["av"],
                                    tq=tq, **attn_cols)
            oa_s = sample_attention(u_s, ks_st, vs_st, l, cache_k, cache_v, l, page_table, scal, attn_norm_w[l],
                                    **attn_cols)
        xp, st_p = _finish_group(xp, u2p, uifp, oa_p, lw, zero_states + (0,), layer=l, depth=depth, prev=prev_p,
                                 chunk=CHUNK if sp % CHUNK == 0 else sp, nb=1, tm=tm_p)
        xs, st_s = _finish_group(xs, u2s, uifs, oa_s, lw, sample_states, layer=l, depth=depth, prev=prev_s,
                                 chunk=CHUNK if ts % CHUNK == 0 else ts, nb=4, tm=tm_s, mlstm_done=done_s)
        prev_p, prev_s = ((kp_st, vp_st), st_p), ((ks_st, vs_st), st_s)

    def unpack(prev, b, t):
        (k_st, v_st), (c, n, m, tail) = prev
        return (k_st.reshape(depth, b, t, heads, dh), v_st.reshape(depth, b, t, heads, dh), c, n, m[..., 0], tail)

    return (xp, xs) + unpack(prev_p, bp, sp) + unpack(prev_s, bs, ts)
```

```python
import functools
import math

import jax
import jax.numpy as jnp
import numpy as np
from jax import lax
from jax.experimental import pallas as pl
from jax.experimental.pallas import tpu as pltpu

F32 = jnp.float32
BF16 = jnp.bfloat16

EPS = 1e-6
CHUNK = 64
LANES = 128
SUBLANES = 8
BF16_ROWS = 16
NEG_BIG = -1e30
VMEM_LIMIT = 56 * 1024 * 1024
VMEM_LIMIT_FUSED = 60 * 1024 * 1024
LOG2E = math.log2(math.e)
KV_GROUP = 2


def _sigmoid(x):
    return jax.nn.sigmoid(x)


def _silu(x):
    return x * _sigmoid(x)


def _params(sem, vmem_limit=VMEM_LIMIT):
    return pltpu.CompilerParams(dimension_semantics=sem, vmem_limit_bytes=vmem_limit)


def _slabs(prev, layer, depth):
    return (depth, 0) if prev is None else (1, layer)


def _stacked_call(kernel_fn, *, grid, in_specs, inputs, out_specs, out_shape, n_stacked, prev, **kw):
    aliases = {}
    if prev is not None:
        first = len(out_shape) - n_stacked
        for k, p in enumerate(prev):
            aliases[len(inputs) + k] = first + k
        in_specs = list(in_specs) + [pl.BlockSpec(memory_space=pl.ANY)] * len(prev)
        inputs = list(inputs) + list(prev)
    return pl.pallas_call(kernel_fn, grid=grid, in_specs=in_specs, out_specs=out_specs, out_shape=out_shape,
                          input_output_aliases=aliases, **kw)(*inputs)


def _dot_nt(a, b):
    return lax.dot_general(a, b, (((1,), (1,)), ((), ())), preferred_element_type=F32)


def _inproj_kernel(x_ref, nw_ref, w_ref, wif_ref, *rest, n_prev, k_blk, v_blk, heads, nslab, slab):
    u_ref, uif_ref, k_hbm, v_hbm, h_scr, k_stage, v_stage, sems = rest[n_prev:]
    i, j = pl.program_id(0), pl.program_id(1)
    tm = x_ref.shape[0]
    dh = k_stage.shape[-1]
    rows = tm * heads
    row0 = pl.multiple_of(i * rows, rows)
    targets = ((k_blk, k_stage, k_hbm), (v_blk, v_stage, v_hbm))

    def copies(which):
        _, stage, hbm = targets[which]
        return [pltpu.make_async_copy(stage, hbm.at[slab + s, pl.ds(row0, rows), :], sems.at[which, s])
                for s in range(nslab)]

    @pl.when(j == 0)
    def _():
        x = x_ref[...]
        ms = jnp.mean(x * x, axis=-1, keepdims=True)
        h = (x * lax.rsqrt(ms + EPS) * nw_ref[...]).astype(BF16)
        h_scr[...] = h
        uif_ref[...] = _dot_nt(h, wif_ref[...].astype(BF16))

    acc = _dot_nt(h_scr[...], w_ref[...])
    u_ref[...] = acc

    for which, (blk, stage, _) in enumerate(targets):
        @pl.when(j == blk)
        def _():
            for h in range(heads):
                stage[pl.ds(h, tm, stride=heads), :] = acc[:, h * dh:(h + 1) * dh]
            for c in copies(which):
                c.start()

    @pl.when(j == pl.num_programs(1) - 1)
    def _():
        for which in range(len(targets)):
            for c in copies(which):
                c.wait()


def in_projection(x, norm_w, w_t, w_if, *, layer, depth, heads, k_col, v_col, prev_kv, tm, tn):
    t, d = x.shape
    n = w_t.shape[1]
    dh = tn // heads
    k_blk, v_blk = k_col // tn, v_col // tn
    assert k_col % tn == 0 and v_col % tn == 0 and tn == heads * dh and max(k_blk, v_blk) < n // tn - 1
    nslab, slab = _slabs(prev_kv, layer, depth)
    kern = functools.partial(_inproj_kernel, n_prev=0 if prev_kv is None else 2, k_blk=k_blk, v_blk=v_blk,
                             heads=heads, nslab=nslab, slab=slab)
    kv_spec = pl.BlockSpec(memory_space=pl.ANY)
    kv_shape = jax.ShapeDtypeStruct((depth, t * heads, dh), F32)
    stage = pltpu.VMEM((tm * heads, dh), F32)
    return _stacked_call(
        kern,
        grid=(t // tm, n // tn),
        in_specs=[
            pl.BlockSpec((tm, d), lambda i, j: (i, 0)),
            pl.BlockSpec((1, d), lambda i, j: (0, 0)),
            pl.BlockSpec((None, tn, d), lambda i, j: (layer, j, 0)),
            pl.BlockSpec((LANES, d), lambda i, j: (0, 0)),
        ],
        inputs=[x, norm_w.reshape(1, d), w_t, w_if],
        out_specs=[
            pl.BlockSpec((tm, tn), lambda i, j: (i, j)),
            pl.BlockSpec((tm, LANES), lambda i, j: (i, 0)),
            kv_spec, kv_spec,
        ],
        out_shape=[jax.ShapeDtypeStruct((t, n), F32), jax.ShapeDtypeStruct((t, LANES), F32), kv_shape, kv_shape],
        n_stacked=2, prev=prev_kv,
        scratch_shapes=[pltpu.VMEM((tm, d), BF16), stage, stage, pltpu.SemaphoreType.DMA((2, nslab))],
        compiler_params=_params(("arbitrary", "arbitrary")),
        name="in_projection",
    )


def _merge_kernel(oa_ref, om_ref, *refs, n_col):
    y_ref = refs[4 * n_col]
    tn = refs[0].shape[1]
    for c in range(n_col):
        wpa_ref, wpm_ref, ga_ref, gm_ref = refs[4 * c:4 * c + 4]
        pa = jnp.dot(oa_ref[...], wpa_ref[...], preferred_element_type=F32)
        pm = jnp.dot(om_ref[...], wpm_ref[...], preferred_element_type=F32)
        y = _sigmoid(ga_ref[...]) * pa + _sigmoid(gm_ref[...]) * pm
        y_ref[:, c * tn:(c + 1) * tn] = y.astype(y_ref.dtype)


def gated_merge(o_a, o_m, w_pa, w_pm, u, ga_col, gm_col, *, tm, tn):
    t, wa = o_a.shape
    wm = o_m.shape[1]
    d = w_pa.shape[1]
    assert ga_col % tn == 0 and gm_col % tn == 0 and d % tn == 0
    n_col = d // tn
    specs, args = [], []
    for c in range(n_col):
        specs += [pl.BlockSpec((wa, tn), lambda i, c=c: (0, c)),
                  pl.BlockSpec((wm, tn), lambda i, c=c: (0, c)),
                  pl.BlockSpec((tm, tn), lambda i, c=c: (i, ga_col // tn + c)),
                  pl.BlockSpec((tm, tn), lambda i, c=c: (i, gm_col // tn + c))]
        args += [w_pa, w_pm, u, u]
    return pl.pallas_call(
        functools.partial(_merge_kernel, n_col=n_col),
        grid=(t // tm,),
        in_specs=[pl.BlockSpec((tm, wa), lambda i: (i, 0)), pl.BlockSpec((tm, wm), lambda i: (i, 0))] + specs,
        out_specs=pl.BlockSpec((tm, d), lambda i: (i, 0)),
        out_shape=jax.ShapeDtypeStruct((t, d), BF16),
        compiler_params=_params(("parallel",)),
        name="gated_merge",
    )(o_a, o_m, *args)


def _outproj_kernel(y_ref, w_ref, nw_ref, x_ref, o_ref):
    out = jnp.dot(y_ref[...], w_ref[...], preferred_element_type=F32)
    ms = jnp.mean(out * out, axis=-1, keepdims=True)
    o_ref[...] = x_ref[...] + out * lax.rsqrt(ms + EPS) * nw_ref[...]


def out_projection(y, w_out, norm_w, x, *, tm):
    t, d = x.shape
    return pl.pallas_call(
        _outproj_kernel,
        grid=(t // tm,),
        in_specs=[
            pl.BlockSpec((tm, d), lambda i: (i, 0)),
            pl.BlockSpec((d, d), lambda i: (0, 0)),
            pl.BlockSpec((1, d), lambda i: (0, 0)),
            pl.BlockSpec((tm, d), lambda i: (i, 0)),
        ],
        out_specs=pl.BlockSpec((tm, d), lambda i: (i, 0)),
        out_shape=jax.ShapeDtypeStruct((t, d), F32),
        compiler_params=_params(("parallel",)),
        name="out_projection",
    )(y, w_out, norm_w.reshape(1, d), x)


def _prompt_attn_kernel(*refs, **kw):
    _prompt_attn_body(pl.program_id(2), *refs, **kw)


def _prompt_attn_body(qi, scal_ref, q_ref, k_ref, v_ref, z_ref, nw_ref, o_ref, kb_scr, vt_scr,
                      *, tq, dqk, scale_log2, post_scale):
    lam = scal_ref[0]
    dv = v_ref.shape[-1]
    n_blk = vt_scr.shape[0]

    @pl.when(qi == 0)
    def _():
        kb_scr[...] = k_ref[0].astype(BF16)
        for c in range(n_blk):
            vt_scr[c, 0:dv, :] = jnp.transpose(v_ref[0, c * tq:(c + 1) * tq, :]).astype(BF16)
            vt_scr[c, dv:, :] = jnp.ones((BF16_ROWS, tq), BF16)

    q = q_ref[0] * scale_log2
    lane = lax.broadcasted_iota(jnp.int32, q.shape, 1)
    q1 = jnp.where(lane < dqk, q, 0.0).astype(BF16)
    q2 = jnp.where(lane >= dqk, q, 0.0).astype(BF16)

    def scores_t(k, qm):
        return lax.dot_general(k, qm, (((1,), (1,)), ((), ())), preferred_element_type=F32)

    def online(state, s_t, vt):
        m, acc = state
        m_new = jnp.maximum(m, jnp.max(s_t, axis=0, keepdims=True))
        alpha = jnp.exp2(m - m_new)
        p = jnp.exp2(s_t - m_new)
        acc = alpha * acc + jnp.dot(vt, p.astype(BF16), preferred_element_type=F32)
        return m_new, acc

    def scores_of(j):
        k = kb_scr[pl.ds(pl.multiple_of(j * tq, tq), tq), :]
        return scores_t(k, q1), scores_t(k, q2)

    def causal(s):
        krow = lax.broadcasted_iota(jnp.int32, s.shape, 0)
        qcol = lax.broadcasted_iota(jnp.int32, s.shape, 1)
        return jnp.where(krow <= qcol, s, -jnp.inf)

    def group(j, carry, n, last_is_diagonal):
        st1, st2 = carry
        scores = [scores_of(j + i) for i in range(n)]
        for i, (s1, s2) in enumerate(scores):
            if last_is_diagonal and i == n - 1:
                s1, s2 = causal(s1), causal(s2)
            vt = vt_scr[j + i]
            st1, st2 = online(st1, s1, vt), online(st2, s2, vt)
        return st1, st2

    def init():
        return jnp.full((1, tq), NEG_BIG, F32), jnp.zeros((dv + BF16_ROWS, tq), F32)

    n_full = qi // KV_GROUP
    carry = lax.fori_loop(0, n_full, lambda g, c: group(KV_GROUP * g, c, KV_GROUP, False), (init(), init()))
    tails = [functools.partial(group, n=r + 1, last_is_diagonal=True) for r in range(KV_GROUP)]
    rest0 = n_full * KV_GROUP
    (_, a1), (_, a2) = lax.switch(qi - rest0, [lambda c, f=f: f(rest0, c) for f in tails], carry)

    out_t = a1[0:dv] / a1[dv:dv + 1] - lam * (a2[0:dv] / a2[dv:dv + 1])
    out = jnp.transpose(out_t)
    ms = jnp.mean(out * out, axis=-1, keepdims=True)
    y = out * lax.rsqrt(ms + EPS) * nw_ref[...] * post_scale
    o_ref[0] = (y * _silu(z_ref[0])).astype(o_ref.dtype)


def prompt_attention(u, scal, norm_w, *, heads, dqk, q_col, k_col, v_col, z_col, post_scale, tq):
    b, s, _ = u.shape
    dh = 2 * dqk
    qb, kb, vb, zb = q_col // dh, k_col // dh, v_col // dh, z_col // dh
    kern = functools.partial(_prompt_attn_kernel, tq=tq, dqk=dqk, scale_log2=dqk ** -0.5 * LOG2E,
                             post_scale=post_scale)
    return pl.pallas_call(
        kern,
        grid=(b, heads, s // tq),
        in_specs=[
            pl.BlockSpec(memory_space=pltpu.SMEM),
            pl.BlockSpec((1, tq, dh), lambda bi, h, i: (bi, i, qb + h)),
            pl.BlockSpec((1, s, dh), lambda bi, h, i: (bi, 0, kb + h)),
            pl.BlockSpec((1, s, dh), lambda bi, h, i: (bi, 0, vb + h)),
            pl.BlockSpec((1, tq, dh), lambda bi, h, i: (bi, i, zb + h)),
            pl.BlockSpec((1, dh), lambda bi, h, i: (0, h)),
        ],
        out_specs=pl.BlockSpec((1, tq, dh), lambda bi, h, i: (bi, i, h)),
        out_shape=jax.ShapeDtypeStruct((b, s, heads * dh), BF16),
        scratch_shapes=[pltpu.VMEM((s, dh), BF16), pltpu.VMEM((s // tq, dh + BF16_ROWS, tq), BF16)],
        compiler_params=_params(("parallel", "parallel", "arbitrary")),
        name="prompt_attention",
    )(scal, u, u, u, u, norm_w.reshape(1, heads * dh))


def _sample_attn_kernel(pt_ref, scal_ref, q_ref, kn_ref, vn_ref, z_ref, nw_ref, bias_ref, biasn_ref, *rest,
                        n_pages, **kw):
    del pt_ref
    _sample_attn_body(scal_ref, q_ref, kn_ref, vn_ref, z_ref, nw_ref, bias_ref, biasn_ref,
                      rest[:n_pages], rest[n_pages:2 * n_pages], rest[2 * n_pages], rest[2 * n_pages + 1], **kw)


def _sample_attn_body(scal_ref, q_ref, kn_ref, vn_ref, z_ref, nw_ref, bias_ref, biasn_ref, k_refs, v_refs,
                      o_ref, s_scr, *, heads, dqk, nq, scale, post_scale):
    n_pages = len(k_refs)
    lam = scal_ref[0]
    dh = 2 * dqk
    rows_pp = k_refs[0].shape[0] * k_refs[0].shape[1]

    q = q_ref[0] * (scale * LOG2E)
    lane = lax.broadcasted_iota(jnp.int32, (nq, dh), 1)
    pieces = []
    for h in range(heads):
        qh = q[:, h * dh:(h + 1) * dh]
        pieces.append(jnp.where(lane < dqk, qh, 0.0))
        pieces.append(jnp.where(lane >= dqk, qh, 0.0))
    qall = jnp.concatenate(pieces, axis=0).astype(BF16)
    nrow = qall.shape[0]

    def scores(k2d):
        return lax.dot_general(qall, k2d, (((1,), (1,)), ((), ())), preferred_element_type=F32)

    bias = bias_ref[...]
    slabs = [slice(c * LANES, (c + 1) * LANES) for c in range(rows_pp // LANES)]
    m_acc = jnp.full((nrow, LANES), NEG_BIG, F32)
    for j in range(n_pages):
        kj = k_refs[j][...].reshape(rows_pp, dh).astype(BF16)
        s = scores(kj)
        for c in slabs:
            sc = s[:, c] + bias
            s_scr[:, j * rows_pp + c.start:j * rows_pp + c.stop] = sc
            m_acc = jnp.maximum(m_acc, sc)
    sn = scores(kn_ref[...].astype(BF16)) + biasn_ref[...]
    m = jnp.maximum(jnp.max(m_acc, axis=-1, keepdims=True), jnp.max(sn, axis=-1, keepdims=True))

    l_acc = jnp.zeros((nrow, LANES), F32)
    acc = jnp.zeros((nrow, dh), F32)
    for j in range(n_pages):
        p = jnp.exp2(s_scr[:, j * rows_pp:(j + 1) * rows_pp] - m)
        for c in slabs:
            l_acc = l_acc + p[:, c]
        vj = v_refs[j][...].reshape(rows_pp, dh).astype(BF16)
        acc = acc + jnp.dot(p.astype(BF16), vj, preferred_element_type=F32)
    pn = jnp.exp2(sn - m)
    l = jnp.sum(l_acc, axis=-1, keepdims=True) + jnp.sum(pn, axis=-1, keepdims=True)
    acc = acc + jnp.dot(pn.astype(BF16), vn_ref[...].astype(BF16), preferred_element_type=F32)

    o = acc / l
    z = z_ref[0]
    nw = nw_ref[...]
    for h in range(heads):
        r0 = h * 2 * nq
        oh = o[r0:r0 + nq] - lam * o[r0 + nq:r0 + 2 * nq]
        ms = jnp.mean(oh * oh, axis=-1, keepdims=True)
        y = oh * lax.rsqrt(ms + EPS) * nw[:, h * dh:(h + 1) * dh] * post_scale
        o_ref[0, :, h * dh:(h + 1) * dh] = (y * _silu(z[:, h * dh:(h + 1) * dh])).astype(o_ref.dtype)


def _sample_masks(heads, nq, page_rows):
    r = np.arange(heads * 2 * nq)
    rh, rt = r // (2 * nq), r % nq
    c = np.arange(page_rows)
    past = np.where((c[None, :] % heads) == rh[:, None], 0.0, -np.inf).astype(np.float32)
    cn = np.arange(nq * heads)
    ok = ((cn[None, :] % heads) == rh[:, None]) & ((cn[None, :] // heads) <= rt[:, None])
    new = np.where(ok, 0.0, -np.inf).astype(np.float32)
    return jnp.asarray(past), jnp.asarray(new)


def sample_attention(u, k_rows, v_rows, kv_layer, cache_k, cache_v, layer, page_table, scal, norm_w,
                     *, dqk, q_col, z_col, post_scale):
    b, nq, _ = u.shape
    _, _, page, heads, dh = cache_k.shape
    n_pages = page_table.shape[1]
    width = heads * dh
    qb, zb = q_col // width, z_col // width
    assert LANES % heads == 0 and (page * heads) % LANES == 0
    bias, bias_new = _sample_masks(heads, nq, LANES)
    nrow = heads * 2 * nq

    def page_spec(j):
        return pl.BlockSpec((None, None, page, heads, dh), lambda i, pt: (layer, pt[i, j], 0, 0, 0))

    new_spec = pl.BlockSpec((None, nq * heads, dh), lambda i, pt: (kv_layer, i, 0))
    kern = functools.partial(_sample_attn_kernel, n_pages=n_pages, heads=heads, dqk=dqk, nq=nq,
                             scale=dqk ** -0.5, post_scale=post_scale)
    grid_spec = pltpu.PrefetchScalarGridSpec(
        num_scalar_prefetch=1,
        grid=(b,),
        in_specs=[
            pl.BlockSpec(memory_space=pltpu.SMEM),
            pl.BlockSpec((1, nq, width), lambda i, pt: (i, 0, qb)),
            new_spec, new_spec,
            pl.BlockSpec((1, nq, width), lambda i, pt: (i, 0, zb)),
            pl.BlockSpec((1, width), lambda i, pt: (0, 0)),
            pl.BlockSpec(bias.shape, lambda i, pt: (0, 0)),
            pl.BlockSpec(bias_new.shape, lambda i, pt: (0, 0)),
        ] + [page_spec(j) for j in range(n_pages)] * 2,
        out_specs=pl.BlockSpec((1, nq, width), lambda i, pt: (i, 0, 0)),
        scratch_shapes=[pltpu.VMEM((nrow, n_pages * page * heads), F32)],
    )
    return pl.pallas_call(
        kern,
        grid_spec=grid_spec,
        out_shape=jax.ShapeDtypeStruct((b, nq, width), BF16),
        compiler_params=_params(("arbitrary",)),
        name="sample_attention",
    )(page_table, scal, u, k_rows, v_rows, u, norm_w.reshape(1, width), bias, bias_new,
      *([cache_k] * n_pages), *([cache_v] * n_pages))


def _fused_attn_kernel(pt_ref, scal_ref, qp_ref, kp_ref, vp_ref, zp_ref, nwp_ref,
                       qs_ref, kn_ref, vn_ref, zs_ref, nws_ref, bias_ref, biasn_ref, *rest,
                       n_pages, n_qblk, heads, dqk, nq, tq, post_scale, n_prev, m_heads, taps):
    del pt_ref
    k_refs, v_refs = rest[:n_pages], rest[n_pages:2 * n_pages]
    rest = rest[2 * n_pages:]
    mlstm_in, rest = rest[:13], rest[13 + n_prev:]
    op_ref, os_ref = rest[:2]
    mlstm_out = rest[2:7]
    kb_scr, vt_scr, s_scr, ext_scr, m_scr = rest[7:]
    _sample_attn_body(scal_ref, qs_ref, kn_ref, vn_ref, zs_ref, nws_ref, bias_ref, biasn_ref, k_refs, v_refs,
                      os_ref, s_scr, heads=heads, dqk=dqk, nq=nq, scale=dqk ** -0.5, post_scale=post_scale)
    _mlstm_body(0, 1, *mlstm_in, *mlstm_out, ext_scr, m_scr, heads=m_heads, taps=taps, single_chunk=True)
    _prompt_attn_body(pl.program_id(0) % n_qblk, scal_ref, qp_ref, kp_ref, vp_ref, zp_ref, nwp_ref, op_ref,
                      kb_scr, vt_scr, tq=tq, dqk=dqk, scale_log2=dqk ** -0.5 * LOG2E, post_scale=post_scale)


def fused_attention(u_p, u_s, k_rows, v_rows, kv_layer, cache_k, cache_v, layer, page_table, scal, norm_w,
                    mlstm_args, mlstm_kw, *, dqk, q_col, k_col, v_col, z_col, post_scale, tq):
    bp, sp, _ = u_p.shape
    bs, nq, _ = u_s.shape
    _, _, page, heads, dh = cache_k.shape
    n_pages = page_table.shape[1]
    width = heads * dh
    n_qblk = sp // tq
    assert bp * heads * n_qblk == bs
    qb, kb, vb, zb = q_col // dh, k_col // dh, v_col // dh, z_col // dh
    qbs, zbs = q_col // width, z_col // width
    assert LANES % heads == 0 and (page * heads) % LANES == 0
    bias, bias_new = _sample_masks(heads, nq, LANES)

    def pb(s):
        return s // (heads * n_qblk)

    def ph(s):
        return (s // n_qblk) % heads

    def pq(s):
        return s % n_qblk

    def page_spec(j):
        return pl.BlockSpec((None, None, page, heads, dh), lambda s, pt: (layer, pt[s, j], 0, 0, 0))

    new_spec = pl.BlockSpec((None, nq * heads, dh), lambda s, pt: (kv_layer, s, 0))
    prev = mlstm_kw["prev"]
    mio = _mlstm_io(*mlstm_args, **mlstm_kw, chunk=nq, nb=1, wrap=lambda f: (lambda s, pt: f(s, 0)))
    n_prev = 0 if prev is None else len(prev)
    kern = functools.partial(_fused_attn_kernel, n_pages=n_pages, n_qblk=n_qblk, heads=heads, dqk=dqk, nq=nq,
                             tq=tq, post_scale=post_scale, n_prev=n_prev, m_heads=mlstm_args[6].shape[2],
                             taps=mlstm_args[2].shape[0])
    nw2 = norm_w.reshape(1, width)
    in_specs = [
        pl.BlockSpec(memory_space=pltpu.SMEM),
        pl.BlockSpec((1, tq, dh), lambda s, pt: (pb(s), pq(s), qb + ph(s))),
        pl.BlockSpec((1, sp, dh), lambda s, pt: (pb(s), 0, kb + ph(s))),
        pl.BlockSpec((1, sp, dh), lambda s, pt: (pb(s), 0, vb + ph(s))),
        pl.BlockSpec((1, tq, dh), lambda s, pt: (pb(s), pq(s), zb + ph(s))),
        pl.BlockSpec((1, dh), lambda s, pt: (0, ph(s))),
        pl.BlockSpec((1, nq, width), lambda s, pt: (s, 0, qbs)),
        new_spec, new_spec,
        pl.BlockSpec((1, nq, width), lambda s, pt: (s, 0, zbs)),
        pl.BlockSpec((1, width), lambda s, pt: (0, 0)),
        pl.BlockSpec(bias.shape, lambda s, pt: (0, 0)),
        pl.BlockSpec(bias_new.shape, lambda s, pt: (0, 0)),
    ] + [page_spec(j) for j in range(n_pages)] * 2 + mio["in_specs"]
    inputs = [scal, u_p, u_p, u_p, u_p, nw2, u_s, k_rows, v_rows, u_s, nw2, bias, bias_new,
              *([cache_k] * n_pages), *([cache_v] * n_pages)] + mio["inputs"]
    aliases = {}
    if prev is not None:
        for k, p in enumerate(prev):
            aliases[1 + len(inputs) + k] = 3 + k
        in_specs = in_specs + [pl.BlockSpec(memory_space=pl.ANY)] * n_prev
        inputs = inputs + list(prev)
    grid_spec = pltpu.PrefetchScalarGridSpec(
        num_scalar_prefetch=1,
        grid=(bs,),
        in_specs=in_specs,
        out_specs=[
            pl.BlockSpec((1, tq, dh), lambda s, pt: (pb(s), pq(s), ph(s))),
            pl.BlockSpec((1, nq, width), lambda s, pt: (s, 0, 0)),
        ] + mio["out_specs"],
        scratch_shapes=[pltpu.VMEM((sp, dh), BF16), pltpu.VMEM((n_qblk, dh + BF16_ROWS, tq), BF16),
                        pltpu.VMEM((heads * 2 * nq, n_pages * page * heads), F32)] + mio["scratch"],
    )
    res = pl.pallas_call(
        kern,
        grid_spec=grid_spec,
        out_shape=[jax.ShapeDtypeStruct((bp, sp, width), BF16), jax.ShapeDtypeStruct((bs, nq, width), BF16)]
        + mio["out_shape"],
        input_output_aliases=aliases,
        compiler_params=_params(("arbitrary",), VMEM_LIMIT_FUSED),
        name="fused_attention",
    )(page_table, *inputs)
    return res[0], res[1], res[2], tuple(res[3:])


def _mlstm_kernel(*refs, n_prev, **kw):
    _mlstm_body(pl.program_id(1), pl.num_programs(1), *refs[:13], *refs[13 + n_prev:], **kw)


def _mlstm_body(ci, nc, qk_ref, v_ref, og_ref, z_ref, if_ref, cw_ref, cb_ref, bif_ref, nw_ref,
                c0_ref, n0_ref, m0_ref, buf0_ref, om_ref, c_ref, n_ref, m_ref, tail_ref, ext_scr, m_scr,
                *, heads, taps, single_chunk):
    nb, L = qk_ref.shape[0], qk_ref.shape[1]
    mw = v_ref.shape[2]
    dh = mw // heads
    pad = SUBLANES
    nslab = c_ref.shape[0]

    if single_chunk:
        for bb in range(nb):
            ext_scr[bb, pad - (taps - 1):pad, :] = buf0_ref[bb]
    else:
        @pl.when(ci == 0)
        def _():
            c_ref[0] = c0_ref[...]
            n_ref[0] = n0_ref[...]
            for bb in range(nb):
                for h in range(heads):
                    m_scr[bb * heads + h] = jnp.broadcast_to(m0_ref[bb, h:h + 1, :], m_scr.shape[1:])
                ext_scr[bb, pad - (taps - 1):pad, :] = buf0_ref[bb]

    row = lax.broadcasted_iota(jnp.int32, (L, L), 0)
    col = lax.broadcasted_iota(jnp.int32, (L, L), 1)
    eye = row == col
    low = col <= row

    for bb in range(nb):
        ext_scr[bb, pad:pad + L, :] = qk_ref[bb]
        conv = cb_ref[...]
        for j in range(taps):
            r0 = pad - (taps - 1) + j
            conv = conv + ext_scr[bb, r0:r0 + L, :] * cw_ref[j:j + 1, :]

        ext_scr[bb, 0:pad, :] = ext_scr[bb, L:L + pad, :]
        qk = _silu(conv)
        q_all = qk[:, :mw]
        k_all = qk[:, mw:] * (dh ** -0.5)
        v_all = v_ref[bb]
        g = if_ref[bb] + bif_ref[...]
        lf_all = jnp.minimum(g, 0.0) - jnp.log(1.0 + jnp.exp(-jnp.abs(g)))
        for h in range(heads):
            li_col = g[:, h:h + 1]
            lf_col = lf_all[:, heads + h:heads + h + 1]
            lf_row = jnp.sum(jnp.where(eye, lf_col, 0.0), axis=0, keepdims=True)
            bt_col = jnp.sum(jnp.where(low, lf_row, 0.0), axis=1, keepdims=True)
            r_row = jnp.sum(jnp.where(eye, li_col - bt_col, 0.0), axis=0, keepdims=True)
            m_prev = m0_ref[bb, h:h + 1, 0:1] if single_chunk else m_scr[bb * heads + h, 0:1, 0:1]

            log_d = jnp.where(low, bt_col + r_row, -jnp.inf)
            log_inter = bt_col + m_prev
            m_t = jnp.maximum(log_inter, jnp.max(log_d, axis=1, keepdims=True))
            d_mat = jnp.exp(log_d - m_t)
            inter = jnp.exp(log_inter - m_t)

            sl = slice(h * dh, (h + 1) * dh)
            q_h, k_h = q_all[:, sl], k_all[:, sl]
            q_b, k_b, v_b = q_h.astype(BF16), k_h.astype(BF16), v_all[:, sl].astype(BF16)
            c_h = c0_ref[bb, h] if single_chunk else c_ref[0, bb, h]
            n_h = n0_ref[bb, h:h + 1, :] if single_chunk else n_ref[0, bb, h:h + 1, :]

            s = lax.dot_general(q_b, k_b, (((1,), (1,)), ((), ())), preferred_element_type=F32) * d_mat
            num = (jnp.dot(s.astype(BF16), v_b, preferred_element_type=F32)
                   + jnp.dot(q_b, c_h.astype(BF16), preferred_element_type=F32) * inter)
            den = jnp.sum(s, axis=1, keepdims=True) + inter * jnp.sum(q_h * n_h, axis=1, keepdims=True)
            den = jnp.maximum(jnp.abs(den), jnp.exp(-m_t))
            hh = num / den

            b_last = jnp.sum(lf_col, axis=0, keepdims=True)
            log_w = b_last - bt_col + li_col
            m_new = jnp.maximum(b_last + m_prev, jnp.max(log_w, axis=0, keepdims=True))
            wk = jnp.exp(log_w - m_new)
            decay = jnp.exp(b_last + m_prev - m_new)
            kw = k_h * wk
            upd = lax.dot_general(kw.astype(BF16), v_b, (((0,), (0,)), ((), ())), preferred_element_type=F32)
            c_new = decay * c_h + upd
            n_new = decay * n_h + jnp.sum(kw, axis=0, keepdims=True)
            for s in range(nslab if single_chunk else 1):
                c_ref[s, bb, h] = c_new
                n_ref[s, bb, h:h + 1, :] = n_new
            m_scr[bb * heads + h] = jnp.broadcast_to(m_new, m_scr.shape[1:])

            mu = jnp.mean(hh, axis=1, keepdims=True)
            xc = hh - mu
            var = jnp.mean(xc * xc, axis=1, keepdims=True)
            y = xc * lax.rsqrt(var + EPS) * nw_ref[:, sl] * _sigmoid(og_ref[bb, :, sl])
            om_ref[bb, :, sl] = (y * _silu(z_ref[bb, :, sl])).astype(om_ref.dtype)

    def finish():
        for s in range(nslab):
            if s > 0 and not single_chunk:
                c_ref[s] = c_ref[0]
                n_ref[s] = n_ref[0]
            for bb in range(nb):
                tail_ref[s, bb] = ext_scr[bb, pad - (taps - 1):pad, :]
                for h in range(heads):
                    m_ref[s, bb, h:h + 1, :] = m_scr[bb * heads + h, 0:1, :]

    if single_chunk:
        finish()
    else:
        pl.when(ci == nc - 1)(finish)


def mlstm_branch(u, u_if, conv_w, conv_b, b_if, norm_w, c0, n0, m0, buf0, *, state_layer, layer, depth, prev,
                 chunk, nb, qk_col, v_col, og_col, z_col):
    b, t, _ = u.shape
    heads = c0.shape[2]
    taps = conv_w.shape[0]
    io = _mlstm_io(u, u_if, conv_w, conv_b, b_if, norm_w, c0, n0, m0, buf0, state_layer=state_layer, layer=layer,
                   depth=depth, prev=prev, chunk=chunk, nb=nb, qk_col=qk_col, v_col=v_col, og_col=og_col,
                   z_col=z_col, wrap=lambda f: f)
    kern = functools.partial(_mlstm_kernel, n_prev=0 if prev is None else 4, heads=heads, taps=taps,
                             single_chunk=(t == chunk))
    res = _stacked_call(
        kern, grid=(b // nb, t // chunk), in_specs=io["in_specs"], inputs=io["inputs"], out_specs=io["out_specs"],
        out_shape=io["out_shape"], n_stacked=4, prev=prev, scratch_shapes=io["scratch"],
        compiler_params=_params(("parallel", "arbitrary")), name="mlstm_branch")
    return res[0], tuple(res[1:])


def _mlstm_io(u, u_if, conv_w, conv_b, b_if, norm_w, c0, n0, m0, buf0, *, state_layer, layer, depth, prev,
              chunk, nb, qk_col, v_col, og_col, z_col, wrap):
    b, t, _ = u.shape
    _, _, heads, dh, _ = c0.shape
    mw = heads * dh
    taps = conv_w.shape[0]
    assert chunk >= taps - 1 and t % chunk == 0 and b % nb == 0
    bif_pad = jnp.zeros((1, LANES), F32).at[0, :2 * heads].set(b_if)
    nslab, slab = _slabs(prev, layer, depth)

    def spec(shape, f):
        return pl.BlockSpec(shape, wrap(f))

    const2 = lambda bi, ci: (0, 0)
    st_in = lambda bi, ci: (state_layer, bi, 0, 0)
    st_out = lambda bi, ci: (slab, bi, 0, 0)
    return dict(
        in_specs=[
            spec((nb, chunk, 2 * mw), lambda bi, ci: (bi, ci, qk_col // (2 * mw))),
            spec((nb, chunk, mw), lambda bi, ci: (bi, ci, v_col // mw)),
            spec((nb, chunk, mw), lambda bi, ci: (bi, ci, og_col // mw)),
            spec((nb, chunk, mw), lambda bi, ci: (bi, ci, z_col // mw)),
            spec((nb, chunk, LANES), lambda bi, ci: (bi, ci, 0)),
            spec((taps, 2 * mw), const2),
            spec((1, 2 * mw), const2),
            spec((1, LANES), const2),
            spec((1, mw), const2),
            spec((None, nb, heads, dh, dh), lambda bi, ci: (state_layer, bi, 0, 0, 0)),
            spec((None, nb, heads, dh), st_in),
            spec((None, nb, heads, LANES), st_in),
            spec((None, nb, taps - 1, 2 * mw), st_in),
        ],
        inputs=[u, u, u, u, u_if, conv_w, conv_b.reshape(1, 2 * mw), bif_pad, norm_w.reshape(1, mw),
                c0, n0, m0, buf0],
        out_specs=[
            spec((nb, chunk, mw), lambda bi, ci: (bi, ci, 0)),
            spec((nslab, nb, heads, dh, dh), lambda bi, ci: (slab, bi, 0, 0, 0)),
            spec((nslab, nb, heads, dh), st_out),
            spec((nslab, nb, heads, LANES), st_out),
            spec((nslab, nb, taps - 1, 2 * mw), st_out),
        ],
        out_shape=[
            jax.ShapeDtypeStruct((b, t, mw), BF16),
            jax.ShapeDtypeStruct((depth, b, heads, dh, dh), F32),
            jax.ShapeDtypeStruct((depth, b, heads, dh), F32),
            jax.ShapeDtypeStruct((depth, b, heads, LANES), F32),
            jax.ShapeDtypeStruct((depth, b, taps - 1, 2 * mw), F32),
        ],
        scratch=[pltpu.VMEM((nb, chunk + SUBLANES, 2 * mw), F32),
                 pltpu.VMEM((nb * heads, SUBLANES, LANES), F32)],
    )


def _wprep_kernel(w_hbm, o_ref, buf, sems, *, n_head_blk, skip):
    l, k = pl.program_id(0), pl.program_id(1)
    nk = pl.num_programs(1)
    tn = o_ref.shape[0]

    def copy(kk, slot):
        off = pl.multiple_of(kk * tn + jnp.where(kk >= n_head_blk, skip, 0), SUBLANES)
        return pltpu.make_async_copy(w_hbm.at[l, pl.ds(off, tn), :], buf.at[slot], sems.at[slot])

    slot = k % 2

    @pl.when(k == 0)
    def _():
        copy(k, slot).start()

    @pl.when(k + 1 < nk)
    def _():
        copy(k + 1, 1 - slot).start()

    copy(k, slot).wait()
    o_ref[...] = buf[slot].astype(o_ref.dtype)


def prepare_in_weight(w_t_all, *, head_rows, skip, tn):
    depth, p_in, d = w_t_all.shape
    n = p_in - skip
    assert head_rows % tn == 0 and n % tn == 0 and skip % SUBLANES == 0
    kern = functools.partial(_wprep_kernel, n_head_blk=head_rows // tn, skip=skip)
    return pl.pallas_call(
        kern,
        grid=(depth, n // tn),
        in_specs=[pl.BlockSpec(memory_space=pl.ANY)],
        out_specs=pl.BlockSpec((None, tn, d), lambda l, k: (l, k, 0)),
        out_shape=jax.ShapeDtypeStruct((depth, n, d), BF16),
        scratch_shapes=[pltpu.VMEM((2, tn, d), F32), pltpu.SemaphoreType.DMA((2,))],
        compiler_params=_params(("arbitrary", "arbitrary")),
        name="prepare_in_weight",
    )(w_t_all)


def _project_in(x, lw, *, layer, depth, prev, tm):
    b, t, d = x.shape
    col = lw["col"]
    return in_projection(
        x.reshape(b * t, d), lw["norm_pre"], lw["w_t"], lw["w_if"], layer=layer, depth=depth,
        heads=lw["heads"], k_col=col["ak"], v_col=col["av"], prev_kv=None if prev is None else prev[0],
        tm=tm, tn=lw["aw"])


def _mlstm_operands(x, u2, uif2, lw, states, *, layer, depth, prev):
    b, t, _ = x.shape
    col = lw["col"]
    c0, n0, m0, buf0, state_layer = states
    args = (u2.reshape(b, t, -1), uif2.reshape(b, t, LANES), lw["conv_w"], lw["conv_b"], lw["b_if"],
            lw["mlstm_norm_w"], c0, n0, m0, buf0)
    kw = dict(state_layer=state_layer, layer=layer, depth=depth, prev=None if prev is None else prev[1],
              qk_col=col["mq"], v_col=col["mv"], og_col=col["mo"], z_col=col["mz"])
    return args, kw


def _finish_group(x, u2, uif2, o_a, lw, states, *, layer, depth, prev, chunk, nb, tm, mlstm_done=None):
    b, t, d = x.shape
    aw, mw, col = lw["aw"], lw["mw"], lw["col"]
    if mlstm_done is None:
        args, kw = _mlstm_operands(x, u2, uif2, lw, states, layer=layer, depth=depth, prev=prev)
        o_m, st = mlstm_branch(*args, **kw, chunk=chunk, nb=nb)
    else:
        o_m, st = mlstm_done
    y = gated_merge(o_a.reshape(b * t, aw), o_m.reshape(b * t, mw), lw["w_pa"], lw["w_pm"], u2,
                    col["ga"], col["gm"], tm=min(tm, 512), tn=1024)
    x_new = out_projection(y, lw["w_out"], lw["norm_post"], x.reshape(b * t, d), tm=min(tm, 512)).reshape(b, t, d)
    return x_new, st


def kernel(x_prompt, x_sample, cache_k, cache_v, state_C, state_n, state_m, state_conv, page_table,
           norm_pre, norm_post, w_in, b_if, conv_w, conv_b, lambda_qk, attn_norm_w, mlstm_norm_w,
           w_pa, w_pm, w_out):
    depth = w_in.shape[0]
    d_model = x_prompt.shape[-1]
    heads = cache_k.shape[3]
    dqk = lambda_qk.shape[-1]
    dh = 2 * dqk
    aw = heads * dh
    m_heads, m_dh = state_C.shape[2], state_C.shape[3]
    mw = m_heads * m_dh
    bp, sp, _ = x_prompt.shape
    bs, ts, _ = x_sample.shape
    taps = conv_w.shape[1]
    n_if = 2 * m_heads
    if_col = 4 * aw + 4 * mw
    col = {"aq": 0, "ak": aw, "av": 2 * aw, "az": 3 * aw, "mq": 4 * aw, "mv": 4 * aw + 2 * mw,
           "mo": 4 * aw + 3 * mw, "mz": if_col, "ga": if_col + mw, "gm": if_col + mw + d_model}

    zero_states = (jnp.zeros((1, bp, m_heads, m_dh, m_dh), F32), jnp.zeros((1, bp, m_heads, m_dh), F32),
                   jnp.zeros((1, bp, m_heads, LANES), F32), jnp.zeros((1, bp, taps - 1, 2 * mw), F32))
    m_rep = jnp.broadcast_to(state_m[..., None], state_m.shape + (LANES,))

    w_t_all = prepare_in_weight(jnp.swapaxes(w_in, 1, 2), head_rows=if_col, skip=n_if, tn=aw)
    tq = 512
    tm_p, tm_s = min(1024, bp * sp), bs * ts
    fuse = bp * heads * (sp // tq) == bs and (ts % CHUNK != 0 or ts == CHUNK)

    xp, xs = x_prompt, x_sample
    prev_p = prev_s = None
    for l in range(depth):
        lam_init = 0.8 - 0.6 * math.exp(-0.3 * l)
        lq = lambda_qk[l].astype(F32)
        lam = jnp.exp(jnp.sum(lq[0] * lq[1])) - jnp.exp(jnp.sum(lq[2] * lq[3])) + lam_init
        scal = jnp.reshape(lam, (1,)).astype(F32)
        w_gates = jnp.swapaxes(w_in[l, :, if_col:if_col + n_if], 0, 1)
        lw = dict(heads=heads, aw=aw, mw=mw, col=col, norm_pre=norm_pre[l], norm_post=norm_post[l],
                  w_t=w_t_all, w_if=jnp.pad(w_gates, ((0, LANES - n_if), (0, 0))),
                  conv_w=conv_w[l], conv_b=conv_b[l], b_if=b_if[l], mlstm_norm_w=mlstm_norm_w[l],
                  w_pa=w_pa[l].astype(BF16), w_pm=w_pm[l].astype(BF16), w_out=w_out[l].astype(BF16))
        attn_cols = dict(dqk=dqk, q_col=col["aq"], z_col=col["az"], post_scale=1.0 - lam_init)

        u2p, uifp, kp_st, vp_st = _project_in(xp, lw, layer=l, depth=depth, prev=prev_p, tm=tm_p)
        u2s, uifs, ks_st, vs_st = _project_in(xs, lw, layer=l, depth=depth, prev=prev_s, tm=tm_s)
        u_p, u_s = u2p.reshape(bp, sp, -1), u2s.reshape(bs, ts, -1)
        sample_states = (state_C, state_n, m_rep, state_conv, l)
        done_s = None
        if fuse:
            m_args, m_kw = _mlstm_operands(xs, u2s, uifs, lw, sample_states, layer=l, depth=depth, prev=prev_s)
            oa_p, oa_s, om_s, st_s = fused_attention(
                u_p, u_s, ks_st, vs_st, l, cache_k, cache_v, l, page_table, scal, attn_norm_w[l], m_args, m_kw,
                k_col=col["ak"], v_col=col["av"], tq=tq, **attn_cols)
            done_s = (om_s, st_s)
        else:
            oa_p = prompt_attention(u_p, scal, attn_norm_w[l], heads=heads, k_col=col["ak"], v_col=col["av"],
                                    tq=tq, **attn_cols)
            oa_s = sample_attention(u_s, ks_st, vs_st, l, cache_k, cache_v, l, page_table, scal, attn_norm_w[l],
                                    **attn_cols)
        xp, st_p = _finish_group(xp, u2p, uifp, oa_p, lw, zero_states + (0,), layer=l, depth=depth, prev=prev_p,
                                 chunk=CHUNK if sp % CHUNK == 0 else sp, nb=1, tm=tm_p)
        xs, st_s = _finish_group(xs, u2s, uifs, oa_s, lw, sample_states, layer=l, depth=depth, prev=prev_s,
                                 chunk=CHUNK if ts % CHUNK == 0 else ts, nb=4, tm=tm_s, mlstm_done=done_s)
        prev_p, prev_s = ((kp_st, vp_st), st_p), ((ks_st, vs_st), st_s)

    def unpack(prev, b, t):
        (k_st, v_st), (c, n, m, tail) = prev
        return (k_st.reshape(depth, b, t, heads, dh), v_st.reshape(depth, b, t, heads, dh), c, n, m[..., 0], tail)

    return (xp, xs) + unpack(prev_p, bp, sp) + unpack(prev_s, bs, ts)
```
